```python
import math
import jax, jax.numpy as jnp
from jax import lax
import numpy as np

D_MODEL = 1024
BATCH = 8
SEQ = 4096
DEPTH = 4

GRID_W = 64
CTX_LEN = 256
N_MOD = 6
RMS_EPS = 1e-6
CHUNK = 128
A_HEADS = 8
A_HEAD_DIM = 64
A_WIDTH = A_HEADS * A_HEAD_DIM
B_WIDTH = 512
C_WIDTH = 512
HY_IN = 3 * C_WIDTH
HY_EMB = 33
HY_FILT = 64
HY_DECAY_LONG_PCT = 1.5
HY_DECAY_SHORT_PCT = 0.3
HY_TARGET = 1e-2
D_HEADS = 4
D_QK = 64
D_V = 2 * D_QK
QK_W = D_HEADS * 2 * D_QK
V_W = D_HEADS * D_V
ATTN_SCALE = D_QK ** -0.5
Q_BLOCK = 128
ROPE_AXIS_DIM = D_QK // 2
ROPE_BASE = 10000.0
AB_IN = 2 * A_WIDTH + 3 * B_WIDTH
AB_OUT = A_WIDTH + B_WIDTH
CD_IN = HY_IN + 2 * QK_W + V_W
CD_OUT = C_WIDTH + V_W
N_KEYS = 128
N_EXPERTS = N_KEYS * N_KEYS
PEER_HEADS = 8
PEER_TOPK = 16
PK_DIM = 128
PEER_BLOCK = 128
N_EVEN = (DEPTH + 1) // 2
N_ODD = DEPTH // 2

kernel_name = 'hybrid_prefix_flow_peer'


def _rmsnorm(x, g):
    xf = x.astype(jnp.float32)
    y = xf * lax.rsqrt(jnp.mean(xf * xf, axis=-1, keepdims=True) + RMS_EPS)
    return (y * g.astype(jnp.float32)).astype(x.dtype)


def _conv3(z, w):
    zp = jnp.pad(z, ((0, 0), (1, 1), (0, 0)))
    return w[0] * zp[:, :-2] + w[1] * zp[:, 1:-1] + w[2] * zp[:, 2:]


def _rot_half(part, cos, sin):
    a, b = jnp.split(part, 2, axis=-1)
    return jnp.concatenate([a * cos - b * sin, b * cos + a * sin], axis=-1)


def _rope_2d(t, cos_r, sin_r, cos_c, sin_c):
    tf = t.astype(jnp.float32)
    t_row, t_col = jnp.split(tf, 2, axis=-1)
    out = jnp.concatenate([_rot_half(t_row, cos_r, sin_r), _rot_half(t_col, cos_c, sin_c)], axis=-1)
    return out.astype(t.dtype)


def _mixer_ab(h, w_in, ws, bs, vnorm_g, conv_w, w_out):
    B, L, _ = h.shape
    p = h @ w_in
    u, v, hb, gb, gc = jnp.split(p, [A_WIDTH, 2 * A_WIDTH, 2 * A_WIDTH + B_WIDTH, 2 * A_WIDTH + 2 * B_WIDTH], axis=-1)
    u = jax.nn.gelu(u, approximate=False)
    v = jax.nn.gelu(v, approximate=False).reshape(B, L // CHUNK, CHUNK, A_HEADS, A_HEAD_DIM)
    v = _rmsnorm(v, vnorm_g.reshape(A_HEADS, A_HEAD_DIM))
    sv = jnp.einsum('hqk,bnkhe->bnqhe', ws, v) + bs.T[:, :, None]
    ya = u * sv.reshape(B, L, A_WIDTH)
    yb = gb * _conv3(gc * hb, conv_w)
    return jnp.concatenate([ya, yb], axis=-1) @ w_out


def _hyena_filter(L, w1, b1, w2, b2, w3, b3, w_out, freq):
    f32 = jnp.float32
    t = jnp.linspace(0.0, 1.0, L, dtype=f32)[:, None]
    bands = (HY_EMB - 1) // 2
    ang = 2.0 * math.pi * jnp.arange(L, dtype=f32)[:, None] / L
    fb = jnp.linspace(1e-4, bands - 1, bands, dtype=f32)[None, :]
    feats = jnp.concatenate([t, jnp.cos(fb * ang), -jnp.sin(fb * ang)], axis=-1)
    fr = freq.astype(f32)
    a = jnp.sin(fr * (feats @ w1.astype(f32) + b1.astype(f32)))
    a = jnp.sin(fr * (a @ w2.astype(f32) + b2.astype(f32)))
    a = jnp.sin(fr * (a @ w3.astype(f32) + b3.astype(f32)))
    hf = a @ w_out.astype(f32)
    deltas = jnp.abs(jnp.linspace(math.log(HY_TARGET) / HY_DECAY_LONG_PCT,
                                  math.log(HY_TARGET) / HY_DECAY_SHORT_PCT, C_WIDTH, dtype=f32))
    decay = jnp.exp(-t * deltas)
    h_fwd = hf[:, :C_WIDTH] * decay
    h_bwd = hf[:, C_WIDTH:] * decay
    filt = jnp.concatenate([h_fwd, jnp.zeros((1, C_WIDTH), f32), h_bwd[:0:-1]], axis=0)
    return filt / jnp.sum(jnp.abs(filt), axis=0, keepdims=True)


def _hyena(z, short_w, short_b, filt, skip):
    B, L, _ = z.shape
    z = _conv3(z, short_w) + short_b
    x0, x1, v = jnp.split(z, 3, axis=-1)
    u = x1 * v
    U = jnp.fft.rfft(u.astype(jnp.float32), n=2 * L, axis=1)
    K = jnp.fft.rfft(filt, axis=0)
    y = jnp.fft.irfft(U * K[None], n=2 * L, axis=1)[:, :L].astype(z.dtype)
    return x0 * (y + u * skip)


def _diff_attention(q, k, v, lam):
    B, Lq = q.shape[0], q.shape[1]
    nb = Lq // Q_BLOCK
    qb = jnp.moveaxis(q.reshape(B, nb, Q_BLOCK, D_HEADS, 2, D_QK), 1, 0)

    def block(qi):
        s = jnp.einsum('bqhid,bkhid->bhiqk', qi, k).astype(jnp.float32) * ATTN_SCALE
        p = jax.nn.softmax(s, axis=-1)
        a = p[:, :, 0] - lam * p[:, :, 1]
        return jnp.einsum('bhqk,bkhe->bqhe', a.astype(v.dtype), v)

    out = lax.map(block, qb)
    return jnp.moveaxis(out, 0, 1).reshape(B, Lq, D_HEADS, D_V)


def _diff_heads(q, k, v, lam, subln_g, lam_init):
    o = _diff_attention(q, k, v, lam)
    o = _rmsnorm(o, subln_g) * (1.0 - lam_init)
    return o.reshape(o.shape[0], o.shape[1], V_W)


def _peer(h, wq, keys, U, V):
    B, L, D = h.shape
    tok = h.reshape((B * L) // PEER_BLOCK, PEER_BLOCK, D)

    def block(xb):
        q = (xb @ wq).reshape(PEER_BLOCK, PEER_HEADS, 2, PK_DIM)
        s = jnp.einsum('thpk,hpnk->thpn', q, keys).astype(jnp.float32)
        top_s, top_i = lax.top_k(s, PEER_TOPK)
        cand_s = (top_s[:, :, 0, :, None] + top_s[:, :, 1, None, :]).reshape(PEER_BLOCK, PEER_HEADS, PEER_TOPK * PEER_TOPK)
        cand_i = (top_i[:, :, 0, :, None] * N_KEYS + top_i[:, :, 1, None, :]).reshape(PEER_BLOCK, PEER_HEADS, PEER_TOPK * PEER_TOPK)
        best_s, pos = lax.top_k(cand_s, PEER_TOPK)
        idx = jnp.take_along_axis(cand_i, pos, axis=-1)
        g = jax.nn.softmax(best_s, axis=-1)
        u = jnp.take(U, idx, axis=0)
        act = jax.nn.gelu(jnp.einsum('thkd,td->thk', u, xb).astype(jnp.float32), approximate=False)
        w = (g * act).astype(xb.dtype)
        vv = jnp.take(V, idx, axis=0)
        return jnp.einsum('thk,thkd->td', w, vv)

    return lax.map(block, tok).reshape(B, L, D)


def setup_inputs(seed: int = 0) -> dict:
    key = jax.random.key(seed)
    ks = list(jax.random.split(key, 64))

    def nrm(shape, scale=1.0):
        return jax.random.normal(ks.pop(), shape, dtype=jnp.float32) * scale

    D = D_MODEL
    return {
        'x': nrm((BATCH, SEQ, D)),
        'c': nrm((BATCH, D)),
        'ctx': nrm((BATCH, CTX_LEN, D)),
        'c_ctx': nrm((D,)),
        'w_mod': nrm((DEPTH, D, N_MOD * D), 0.5 * D ** -0.5),
        'b_mod': nrm((DEPTH, N_MOD * D), 0.01),
        'norm1_g': 1.0 + nrm((DEPTH, D), 0.05),
        'norm2_g': 1.0 + nrm((DEPTH, D), 0.05),
        'ab_w_in': nrm((N_EVEN, D, AB_IN), D ** -0.5),
        'a_ws': nrm((N_EVEN, A_HEADS, CHUNK, CHUNK), CHUNK ** -0.5),
        'a_bs': 1.0 + nrm((N_EVEN, A_HEADS, CHUNK), 0.05),
        'a_vnorm_g': 1.0 + nrm((N_EVEN, A_WIDTH), 0.05),
        'b_conv_w': nrm((N_EVEN, 3, B_WIDTH), 3 ** -0.5),
        'ab_w_out': nrm((N_EVEN, AB_OUT, D), AB_OUT ** -0.5),
        'cd_w_in': nrm((N_ODD, D, CD_IN), D ** -0.5),
        'c_short_w': nrm((N_ODD, 3, HY_IN), 3 ** -0.5),
        'c_short_b': nrm((N_ODD, HY_IN), 0.01),
        'c_filt_w1': nrm((N_ODD, HY_EMB, HY_FILT), HY_EMB ** -0.5),
        'c_filt_b1': nrm((N_ODD, HY_FILT), 0.1),
        'c_filt_w2': nrm((N_ODD, HY_FILT, HY_FILT), HY_FILT ** -0.5),
        'c_filt_b2': nrm((N_ODD, HY_FILT), 0.1),
        'c_filt_w3': nrm((N_ODD, HY_FILT, HY_FILT), HY_FILT ** -0.5),
        'c_filt_b3': nrm((N_ODD, HY_FILT), 0.1),
        'c_filt_wout': nrm((N_ODD, HY_FILT, 2 * C_WIDTH), HY_FILT ** -0.5),
        'c_filt_freq': 1.0 + nrm((N_ODD, HY_FILT), 0.05),
        'c_skip': nrm((N_ODD, C_WIDTH), 1.0),
        'd_q_norm_g': 1.0 + nrm((N_ODD, D_QK), 0.05),
        'd_k_norm_g': 1.0 + nrm((N_ODD, D_QK), 0.05),
        'd_lambda_q1': nrm((N_ODD, D_QK), 0.1),
        'd_lambda_k1': nrm((N_ODD, D_QK), 0.1),
        'd_lambda_q2': nrm((N_ODD, D_QK), 0.1),
        'd_lambda_k2': nrm((N_ODD, D_QK), 0.1),
        'd_subln_g': 1.0 + nrm((N_ODD, D_V), 0.05),
        'cd_w_out': nrm((N_ODD, CD_OUT, D), CD_OUT ** -0.5),
        'peer_wq': nrm((DEPTH, D, PEER_HEADS * 2 * PK_DIM), D ** -0.5),
        'peer_keys': nrm((DEPTH, PEER_HEADS, 2, N_KEYS, PK_DIM), PK_DIM ** -0.5),
        'peer_u': nrm((DEPTH, N_EXPERTS, D), D ** -0.5),
        'peer_v': nrm((DEPTH, N_EXPERTS, D), 0.5),
    }


def reference(x, c, ctx, c_ctx, w_mod, b_mod, norm1_g, norm2_g,
              ab_w_in, a_ws, a_bs, a_vnorm_g, b_conv_w, ab_w_out,
              cd_w_in, c_short_w, c_short_b, c_filt_w1, c_filt_b1, c_filt_w2, c_filt_b2,
              c_filt_w3, c_filt_b3, c_filt_wout, c_filt_freq, c_skip,
              d_q_norm_g, d_k_norm_g, d_lambda_q1, d_lambda_k1, d_lambda_q2, d_lambda_k2,
              d_subln_g, cd_w_out, peer_wq, peer_keys, peer_u, peer_v):
    f32 = jnp.float32
    B, L, _ = x.shape
    Lc = ctx.shape[1]
    ROWS = L // GRID_W
    rows = jnp.repeat(jnp.arange(ROWS, dtype=f32), GRID_W)
    cols = jnp.tile(jnp.arange(GRID_W, dtype=f32), ROWS)
    inv = ROPE_BASE ** (-jnp.arange(0, ROPE_AXIS_DIM, 2, dtype=f32) / ROPE_AXIS_DIM)
    ang_r = (rows[:, None] * inv)[:, None, None, :]
    ang_c = (cols[:, None] * inv)[:, None, None, :]
    rope = (jnp.cos(ang_r), jnp.sin(ang_r), jnp.cos(ang_c), jnp.sin(ang_c))

    s_lat = jax.nn.silu(c)
    s_ctx = jax.nn.silu(c_ctx)[None]
    xc = ctx
    for l in range(DEPTH):
        last = l == DEPTH - 1
        odd = l % 2 == 1
        i = l // 2
        mod = (s_lat @ w_mod[l] + b_mod[l])[:, None, :]
        modc = (s_ctx @ w_mod[l] + b_mod[l])[:, None, :]
        sh1, sc1, g1, sh2, sc2, g2 = jnp.split(mod, N_MOD, axis=-1)
        sh1c, sc1c, g1c, sh2c, sc2c, g2c = jnp.split(modc, N_MOD, axis=-1)
        h = _rmsnorm(x, norm1_g[l]) * (1.0 + sc1) + sh1
        if (not last) or odd:
            hc = _rmsnorm(xc, norm1_g[l]) * (1.0 + sc1c) + sh1c
        if not odd:
            ab = (ab_w_in[i], a_ws[i], a_bs[i], a_vnorm_g[i], b_conv_w[i], ab_w_out[i])
            x = x + g1 * _mixer_ab(h, *ab)
            if not last:
                xc = xc + g1c * _mixer_ab(hc, *ab)
        else:
            lam_init = 0.8 - 0.6 * math.exp(-0.3 * l)
            lam = (jnp.exp(jnp.sum(d_lambda_q1[i].astype(f32) * d_lambda_k1[i].astype(f32)))
                   - jnp.exp(jnp.sum(d_lambda_q2[i].astype(f32) * d_lambda_k2[i].astype(f32)))
                   + lam_init)
            filt_p = (c_filt_w1[i], c_filt_b1[i], c_filt_w2[i], c_filt_b2[i],
                      c_filt_w3[i], c_filt_b3[i], c_filt_wout[i], c_filt_freq[i])
            if last:
                kc, vc = jnp.split(hc @ cd_w_in[i][:, HY_IN + QK_W:], [QK_W], axis=-1)
            else:
                hyc, qc, kc, vc = jnp.split(hc @ cd_w_in[i], [HY_IN, HY_IN + QK_W, HY_IN + 2 * QK_W], axis=-1)
            kc = _rmsnorm(kc.reshape(B, Lc, D_HEADS, 2, D_QK), d_k_norm_g[i])
            vc = vc.reshape(B, Lc, D_HEADS, D_V)
            hy, q, k, v = jnp.split(h @ cd_w_in[i], [HY_IN, HY_IN + QK_W, HY_IN + 2 * QK_W], axis=-1)
            q = _rope_2d(_rmsnorm(q.reshape(B, L, D_HEADS, 2, D_QK), d_q_norm_g[i]), *rope)
            k = _rope_2d(_rmsnorm(k.reshape(B, L, D_HEADS, 2, D_QK), d_k_norm_g[i]), *rope)
            v = v.reshape(B, L, D_HEADS, D_V)
            yd = _diff_heads(q, jnp.concatenate([kc, k], axis=1), jnp.concatenate([vc, v], axis=1),
                             lam, d_subln_g[i], lam_init)
            yh = _hyena(hy, c_short_w[i], c_short_b[i], _hyena_filter(L, *filt_p), c_skip[i])
            x = x + g1 * (jnp.concatenate([yh, yd], axis=-1) @ cd_w_out[i])
            if not last:
                qc = _rmsnorm(qc.reshape(B, Lc, D_HEADS, 2, D_QK), d_q_norm_g[i])
                ydc = _diff_heads(qc, kc, vc, lam, d_subln_g[i], lam_init)
                yhc = _hyena(hyc, c_short_w[i], c_short_b[i], _hyena_filter(Lc, *filt_p), c_skip[i])
                xc = xc + g1c * (jnp.concatenate([yhc, ydc], axis=-1) @ cd_w_out[i])
        h2 = _rmsnorm(x, norm2_g[l]) * (1.0 + sc2) + sh2
        x = x + g2 * _peer(h2, peer_wq[l], peer_keys[l], peer_u[l], peer_v[l])
        if not last:
            h2c = _rmsnorm(xc, norm2_g[l]) * (1.0 + sc2c) + sh2c
            xc = xc + g2c * _peer(h2c, peer_wq[l], peer_keys[l], peer_u[l], peer_v[l])
    return x
```

```python
import functools
import math

import jax
import jax.numpy as jnp
from jax import lax
from jax.experimental import pallas as pl
from jax.experimental.pallas import tpu as pltpu

F32 = jnp.float32
BF16 = jnp.bfloat16
HI = lax.Precision.HIGHEST

GRID_W = 64
N_MOD = 6
RMS_EPS = 1e-6
CHUNK = 128
A_HEADS = 8
A_HEAD_DIM = 64
A_WIDTH = A_HEADS * A_HEAD_DIM
B_WIDTH = 512
C_WIDTH = 512
HY_IN = 3 * C_WIDTH
HY_EMB = 33
HY_DECAY_LONG_PCT = 1.5
HY_DECAY_SHORT_PCT = 0.3
HY_TARGET = 1e-2
D_HEADS = 4
D_QK = 64
D_V = 2 * D_QK
QK_W = D_HEADS * 2 * D_QK
V_W = D_HEADS * D_V
ATTN_SCALE = D_QK ** -0.5
ROPE_AXIS_DIM = D_QK // 2
ROPE_BASE = 10000.0
PEER_HEADS = 8
PEER_TOPK = 16

LANES = 128
SUBLANES = 8
VMEM_LIMIT = 56 * 1024 * 1024


def _cparams(*sem):
    return pltpu.CompilerParams(dimension_semantics=sem, vmem_limit_bytes=VMEM_LIMIT)


def _gelu(x):
    return 0.5 * x * (1.0 + lax.erf(x * (2.0 ** -0.5)))


def _group_mean_sq(v, group):
    w = v.shape[-1]
    r = lax.broadcasted_iota(jnp.int32, (w, w), 0) // group
    c = lax.broadcasted_iota(jnp.int32, (w, w), 1) // group
    bd = jnp.where(r == c, 1.0, 0.0).astype(BF16)
    vv = v * v
    hi = vv.astype(BF16)
    lo = (vv - hi.astype(F32)).astype(BF16)
    s = jnp.dot(hi, bd, preferred_element_type=F32) + jnp.dot(lo, bd, preferred_element_type=F32)
    return s * (1.0 / group)


def _mod_kernel(s_ref, w_ref, b_ref, o_ref):
    s = s_ref[...]
    s = s / (1.0 + jnp.exp(-s))
    o_ref[0] = jnp.dot(s, w_ref[0], precision=HI, preferred_element_type=F32) + b_ref[0]


def _mod_all(s_rows, w_mod, b_mod):
    depth, d, n6 = w_mod.shape
    rows = s_rows.shape[0]
    tn = n6 // 4
    return pl.pallas_call(
        _mod_kernel,
        grid=(depth, n6 // tn),
        in_specs=[
            pl.BlockSpec((rows, d), lambda l, j: (0, 0)),
            pl.BlockSpec((1, d, tn), lambda l, j: (l, 0, j)),
            pl.BlockSpec((1, 1, tn), lambda l, j: (l, 0, j)),
        ],
        out_specs=pl.BlockSpec((1, rows, tn), lambda l, j: (l, 0, j)),
        out_shape=jax.ShapeDtypeStruct((depth, rows, n6), F32),
        compiler_params=_cparams("arbitrary", "arbitrary"),
        name="mod",
    )(s_rows, w_mod, b_mod.reshape(depth, 1, n6))


def _norm_mod(x, g, sc, sh):
    ms = jnp.mean(x * x, axis=-1, keepdims=True)
    return (x * lax.rsqrt(ms + RMS_EPS) * g) * (1.0 + sc) + sh


def _nmm_kernel(x_ref, sc_ref, sh_ref, g_ref, w_ref, o_ref):
    h = _norm_mod(x_ref[...], g_ref[...], sc_ref[0], sh_ref[0])
    o_ref[...] = jnp.dot(h.astype(BF16), w_ref[...], preferred_element_type=F32)


def _norm_mod_matmul(x, sc, sh, g, w, row_of, tm):
    n, d = x.shape
    nout = w.shape[1]
    return pl.pallas_call(
        _nmm_kernel,
        grid=(n // tm,),
        in_specs=[
            pl.BlockSpec((tm, d), lambda i: (i, 0)),
            pl.BlockSpec((1, 1, d), lambda i: (row_of(i), 0, 0)),
            pl.BlockSpec((1, 1, d), lambda i: (row_of(i), 0, 0)),
            pl.BlockSpec((1, d), lambda i: (0, 0)),
            pl.BlockSpec((d, nout), lambda i: (0, 0)),
        ],
        out_specs=pl.BlockSpec((tm, nout), lambda i: (i, 0)),
        out_shape=jax.ShapeDtypeStruct((n, nout), F32),
        compiler_params=_cparams("parallel"),
        name="norm_mod_inproj",
    )(x, sc, sh, g, w)


def _shift_rows(z, prev_row, next_row):
    tm = z.shape[0]
    row = lax.broadcasted_iota(jnp.int32, z.shape, 0)
    up = jnp.where(row == 0, prev_row, pltpu.roll(z, 1, 0))
    dn = jnp.where(row == tm - 1, next_row, pltpu.roll(z, tm - 1, 0))
    return up, dn


def _mix_ab_kernel(p_ref, hbp_ref, gcp_ref, hbn_ref, gcn_ref, x_ref, g1_ref, ws_ref, bias_ref, vg_ref, cw_ref,
                   wout_ref, o_ref, *, tiles_per_seq):
    i = pl.program_id(0)
    tm = p_ref.shape[0]
    aw, bw = A_WIDTH, B_WIDTH
    u = _gelu(p_ref[:, 0:aw])
    v = _gelu(p_ref[:, aw:2 * aw])
    vn = (v * lax.rsqrt(_group_mean_sq(v, A_HEAD_DIM) + RMS_EPS) * vg_ref[...]).astype(BF16)
    sv_chunks = []
    for cidx in range(tm // CHUNK):
        vc = vn[cidx * CHUNK:(cidx + 1) * CHUNK]
        heads = [jnp.dot(ws_ref[h], vc[:, h * A_HEAD_DIM:(h + 1) * A_HEAD_DIM], preferred_element_type=F32)
                 for h in range(A_HEADS)]
        sv_chunks.append(jnp.concatenate(heads, axis=-1) + bias_ref[...])
    sv = jnp.concatenate(sv_chunks, axis=0) if len(sv_chunks) > 1 else sv_chunks[0]
    ya = u * sv

    o0 = 2 * aw
    hb = p_ref[:, o0:o0 + bw]
    gb = p_ref[:, o0 + bw:o0 + 2 * bw]
    gc = p_ref[:, o0 + 2 * bw:o0 + 3 * bw]
    z = gc * hb
    pos = i % tiles_per_seq
    zp = jnp.where(pos == 0, 0.0, gcp_ref[SUBLANES - 1:SUBLANES, :] * hbp_ref[SUBLANES - 1:SUBLANES, :])
    zn = jnp.where(pos == tiles_per_seq - 1, 0.0, gcn_ref[0:1, :] * hbn_ref[0:1, :])
    z_up, z_dn = _shift_rows(z, zp, zn)
    yb = gb * (cw_ref[0:1, :] * z_up + cw_ref[1:2, :] * z + cw_ref[2:3, :] * z_dn)

    y = jnp.concatenate([ya, yb], axis=-1).astype(BF16)
    o_ref[...] = x_ref[...] + g1_ref[0] * jnp.dot(y, wout_ref[...], preferred_element_type=F32)


def _mix_ab(p, x, g1, ws, bias_full, vg, cw, w_out, row_of, seq_len, tm):
    n, d = x.shape
    nin = p.shape[1]
    nb8 = n // SUBLANES
    r8 = tm // SUBLANES
    hb_blk = (2 * A_WIDTH) // B_WIDTH
    gc_blk = hb_blk + 2
    prev = lambda i: jnp.maximum(i * r8 - 1, 0)
    nxt = lambda i: jnp.minimum((i + 1) * r8, nb8 - 1)
    full = lambda a: pl.BlockSpec(a.shape, lambda i: (0,) * a.ndim)
    return pl.pallas_call(
        functools.partial(_mix_ab_kernel, tiles_per_seq=seq_len // tm),
        grid=(n // tm,),
        in_specs=[
            pl.BlockSpec((tm, nin), lambda i: (i, 0)),
            pl.BlockSpec((SUBLANES, B_WIDTH), lambda i: (prev(i), hb_blk)),
            pl.BlockSpec((SUBLANES, B_WIDTH), lambda i: (prev(i), gc_blk)),
            pl.BlockSpec((SUBLANES, B_WIDTH), lambda i: (nxt(i), hb_blk)),
            pl.BlockSpec((SUBLANES, B_WIDTH), lambda i: (nxt(i), gc_blk)),
            pl.BlockSpec((tm, d), lambda i: (i, 0)),
            pl.BlockSpec((1, 1, d), lambda i: (row_of(i), 0, 0)),
            full(ws), full(bias_full), full(vg), full(cw), full(w_out),
        ],
        out_specs=pl.BlockSpec((tm, d), lambda i: (i, 0)),
        out_shape=jax.ShapeDtypeStruct((n, d), F32),
        compiler_params=_cparams("parallel"),
        name="mix_ab",
    )(p, p, p, p, p, x, g1, ws, bias_full, vg, cw, w_out)


def _qk_prep_kernel(q_ref, k_ref, cos_ref, sin_ref, qg_ref, kg_ref, qm_ref, ko_ref):
    lane = lax.broadcasted_iota(jnp.int32, q_ref.shape, 1)
    first = (lane % ROPE_AXIS_DIM) < (ROPE_AXIS_DIM // 2)
    w = q_ref.shape[1]
    half = ROPE_AXIS_DIM // 2

    def prep(t, g):
        t = t * lax.rsqrt(_group_mean_sq(t, D_QK) + RMS_EPS) * g
        partner = jnp.where(first, pltpu.roll(t, w - half, 1), pltpu.roll(t, half, 1))
        return t * cos_ref[...] + partner * sin_ref[...]

    q = prep(q_ref[...], qg_ref[...]) * ATTN_SCALE
    in_h0 = (lane % (2 * D_QK)) < D_QK
    qm_ref[0] = jnp.where(in_h0, q, 0.0).astype(BF16)
    qm_ref[1] = jnp.where(in_h0, 0.0, q).astype(BF16)
    ko_ref[...] = prep(k_ref[...], kg_ref[...]).astype(BF16)


def _qk_prep(p, cos_t, sin_t, qg, kg, tab_of, tm):
    n = p.shape[0]
    q_blk = HY_IN // QK_W
    return pl.pallas_call(
        _qk_prep_kernel,
        grid=(n // tm,),
        in_specs=[
            pl.BlockSpec((tm, QK_W), lambda i: (i, q_blk)),
            pl.BlockSpec((tm, QK_W), lambda i: (i, q_blk + 1)),
            pl.BlockSpec((tm, QK_W), lambda i: (tab_of(i), 0)),
            pl.BlockSpec((tm, QK_W), lambda i: (tab_of(i), 0)),
            pl.BlockSpec((1, QK_W), lambda i: (0, 0)),
            pl.BlockSpec((1, QK_W), lambda i: (0, 0)),
        ],
        out_specs=[
            pl.BlockSpec((2, tm, QK_W), lambda i: (0, i, 0)),
            pl.BlockSpec((tm, QK_W), lambda i: (i, 0)),
        ],
        out_shape=[jax.ShapeDtypeStruct((2, n, QK_W), BF16), jax.ShapeDtypeStruct((n, QK_W), BF16)],
        compiler_params=_cparams("parallel"),
        name="qk_prep",
    )(p, p, cos_t, sin_t, qg, kg)


def _diff_attn_kernel(q0_ref, q1_ref, k_ref, v_ref, lq1_ref, lk1_ref, lq2_ref, lk2_ref, sg_ref, o_ref, *, lam_init):
    lam = (jnp.exp(jnp.sum(lq1_ref[...] * lk1_ref[...], axis=-1, keepdims=True))
           - jnp.exp(jnp.sum(lq2_ref[...] * lk2_ref[...], axis=-1, keepdims=True)) + lam_init)
    k = k_ref[0]
    v = v_ref[0]

    def half(q):
        s = lax.dot_general(q, k, (((1,), (1,)), ((), ())), preferred_element_type=F32)
        m = jnp.max(s, axis=-1, keepdims=True)
        p = jnp.exp(s - m)
        l = jnp.sum(p, axis=-1, keepdims=True)
        return jnp.dot(p.astype(BF16), v, preferred_element_type=F32) / l

    o = half(q0_ref[0]) - lam * half(q1_ref[0])
    ms = jnp.mean(o * o, axis=-1, keepdims=True)
    o_ref[...] = (o * lax.rsqrt(ms + RMS_EPS) * sg_ref[...]) * (1.0 - lam_init)


def _diff_attn(qm, k_all, v_all, lam_vecs, subln_g, lam_init, batch, lq, tq):
    lk = k_all.shape[1]
    nq = lq // tq
    vec = lambda a: pl.BlockSpec(a.shape, lambda b, h, i: (0, 0))
    return pl.pallas_call(
        functools.partial(_diff_attn_kernel, lam_init=lam_init),
        grid=(batch, D_HEADS, nq),
        in_specs=[
            pl.BlockSpec((1, tq, D_V), lambda b, h, i: (0, b * nq + i, h)),
            pl.BlockSpec((1, tq, D_V), lambda b, h, i: (1, b * nq + i, h)),
            pl.BlockSpec((1, lk, D_V), lambda b, h, i: (b, 0, h)),
            pl.BlockSpec((1, lk, D_V), lambda b, h, i: (b, 0, h)),
            vec(lam_vecs[0]), vec(lam_vecs[1]), vec(lam_vecs[2]), vec(lam_vecs[3]), vec(subln_g),
        ],
        out_specs=pl.BlockSpec((tq, D_V), lambda b, h, i: (b * nq + i, h)),
        out_shape=jax.ShapeDtypeStruct((batch * lq, V_W), F32),
        compiler_params=_cparams("parallel", "parallel", "arbitrary"),
        name="diff_attn",
    )(qm, qm, k_all, v_all, *lam_vecs, subln_g)


def _hy_pre_kernel(p_ref, pp_ref, pn_ref, sw_ref, sb_ref, x0_ref, u_ref, ub_ref, *, tiles_per_seq):
    i = pl.program_id(0)
    pos = i % tiles_per_seq
    z = p_ref[...]
    zp = jnp.where(pos == 0, 0.0, pp_ref[SUBLANES - 1:SUBLANES, :])
    zn = jnp.where(pos == tiles_per_seq - 1, 0.0, pn_ref[0:1, :])
    z_up, z_dn = _shift_rows(z, zp, zn)
    c = sw_ref[0:1, :] * z_up + sw_ref[1:2, :] * z + sw_ref[2:3, :] * z_dn + sb_ref[...]
    cw = C_WIDTH
    u = c[:, cw:2 * cw] * c[:, 2 * cw:3 * cw]
    x0_ref[...] = c[:, 0:cw]
    u_ref[...] = u
    ub_ref[...] = u.astype(BF16)


def _hy_pre(p, sw, sb, seq_len, tm):
    n = p.shape[0]
    nb8 = n // SUBLANES
    r8 = tm // SUBLANES
    prev = lambda i: jnp.maximum(i * r8 - 1, 0)
    nxt = lambda i: jnp.minimum((i + 1) * r8, nb8 - 1)
    o_spec = pl.BlockSpec((tm, C_WIDTH), lambda i: (i, 0))
    return pl.pallas_call(
        functools.partial(_hy_pre_kernel, tiles_per_seq=seq_len // tm),
        grid=(n // tm,),
        in_specs=[
            pl.BlockSpec((tm, HY_IN), lambda i: (i, 0)),
            pl.BlockSpec((SUBLANES, HY_IN), lambda i: (prev(i), 0)),
            pl.BlockSpec((SUBLANES, HY_IN), lambda i: (nxt(i), 0)),
            pl.BlockSpec((3, HY_IN), lambda i: (0, 0)),
            pl.BlockSpec((1, HY_IN), lambda i: (0, 0)),
        ],
        out_specs=[o_spec, o_spec, o_spec],
        out_shape=[jax.ShapeDtypeStruct((n, C_WIDTH), F32), jax.ShapeDtypeStruct((n, C_WIDTH), F32),
                   jax.ShapeDtypeStruct((n, C_WIDTH), BF16)],
        compiler_params=_cparams("parallel"),
        name="hyena_pre",
    )(p, p, p, sw, sb)


def _hy_filter_kernel(w1_ref, b1_ref, w2_ref, b2_ref, w3_ref, b3_ref, wo_ref, fr_ref, dl_ref, o_ref):
    L = o_ref.shape[0]
    fw = w1_ref.shape[0]
    bands = (HY_EMB - 1) // 2
    row = lax.broadcasted_iota(jnp.int32, (L, fw), 0).astype(F32)
    col = lax.broadcasted_iota(jnp.int32, (L, fw), 1)
    t = row / (L - 1.0)
    ang = (2.0 * math.pi) * row / L
    step = (bands - 1 - 1e-4) / (bands - 1)
    band = jnp.where(col <= bands, col - 1, col - 1 - bands).astype(F32)
    arg = (1e-4 + band * step) * ang
    feats = jnp.where(col == 0, t, jnp.where(col <= bands, jnp.cos(arg), jnp.where(col < HY_EMB, -jnp.sin(arg), 0.0)))
    fr = fr_ref[...]
    mm = lambda a, w: jnp.dot(a, w, precision=HI, preferred_element_type=F32)
    a = jnp.sin(fr * (mm(feats, w1_ref[...]) + b1_ref[...]))
    a = jnp.sin(fr * (mm(a, w2_ref[...]) + b2_ref[...]))
    a = jnp.sin(fr * (mm(a, w3_ref[...]) + b3_ref[...]))
    hf = mm(a, wo_ref[...])
    rowc = lax.broadcasted_iota(jnp.int32, (L, C_WIDTH), 0)
    tc = rowc.astype(F32) / (L - 1.0)
    decay = jnp.exp(-tc * dl_ref[...])
    h_fwd = hf[:, :C_WIDTH] * decay
    h_bwd = jnp.where(rowc == 0, 0.0, hf[:, C_WIDTH:] * decay)
    norm = jnp.sum(jnp.abs(h_fwd) + jnp.abs(h_bwd), axis=0, keepdims=True)
    o_ref[:, :C_WIDTH] = (h_fwd / norm).astype(BF16)
    o_ref[:, C_WIDTH:] = (h_bwd / norm).astype(BF16)


def _hy_filter(L, fp, deltas):
    full = lambda a: pl.BlockSpec(a.shape, lambda: (0,) * a.ndim)
    args = (*fp, deltas)
    return pl.pallas_call(
        _hy_filter_kernel,
        in_specs=[full(a) for a in args],
        out_specs=pl.BlockSpec((L, 2 * C_WIDTH), lambda: (0, 0)),
        out_shape=jax.ShapeDtypeStruct((L, 2 * C_WIDTH), BF16),
        compiler_params=pltpu.CompilerParams(vmem_limit_bytes=VMEM_LIMIT),
        name="hyena_filter",
    )(*args)


def _mm_kernel(a_ref, b_ref, o_ref):
    o_ref[...] = jnp.dot(a_ref[...], b_ref[...], preferred_element_type=F32)


def _matmul(a, b, tm):
    m, k = a.shape
    n = b.shape[1]
    return pl.pallas_call(
        _mm_kernel,
        grid=(m // tm,),
        in_specs=[pl.BlockSpec((tm, k), lambda i: (i, 0)), pl.BlockSpec((k, n), lambda i: (0, 0))],
        out_specs=pl.BlockSpec((tm, n), lambda i: (i, 0)),
        out_shape=jax.ShapeDtypeStruct((m, n), F32),
        compiler_params=_cparams("parallel"),
        name="dft_filter",
    )(a, b)


def _hy_conv_kernel(u_ref, wt_ref, wb_ref, w2t_ref, w2b_ref, at_ref, ab_ref, y_ref):
    j = pl.program_id(1)
    cw = C_WIDTH
    u = u_ref[...]
    uc = jnp.dot(wt_ref[...], u, preferred_element_type=F32)
    us = jnp.dot(wb_ref[...], u, preferred_element_type=F32)
    kre = at_ref[:, :cw] + at_ref[:, cw:]
    kim = ab_ref[:, cw:] - ab_ref[:, :cw]
    knyq = ab_ref[:, :cw] + ab_ref[:, cw:]
    row = lax.broadcasted_iota(jnp.int32, kre.shape, 0)
    special = jnp.logical_and(row == 0, j == 0)
    p2 = jnp.where(special, 0.0, kim)
    p4 = jnp.where(special, knyq, -kre)
    yt = (uc * kre + us * p2).astype(BF16)
    yb = (uc * p2 + us * p4).astype(BF16)
    L = y_ref.shape[0]
    tr = min(L, 512)

    def inverse_rows(rows, first):
        contrib = (jnp.dot(w2t_ref[rows, :], yt, preferred_element_type=F32)
                   + jnp.dot(w2b_ref[rows, :], yb, preferred_element_type=F32))
        y_ref[rows, :] = contrib if first else y_ref[rows, :] + contrib

    @pl.when(j == 0)
    def _():
        for c in range(L // tr):
            inverse_rows(slice(c * tr, (c + 1) * tr), True)

    @pl.when(j > 0)
    def _():
        for c in range(L // tr):
            inverse_rows(slice(c * tr, (c + 1) * tr), False)


def _hy_conv(ub, w, w2, a_spec, batch, L, rb):
    ns = L // rb
    return pl.pallas_call(
        _hy_conv_kernel,
        grid=(batch, ns),
        in_specs=[
            pl.BlockSpec((L, C_WIDTH), lambda b, j: (b, 0)),
            pl.BlockSpec((rb, L), lambda b, j: (j, 0)),
            pl.BlockSpec((rb, L), lambda b, j: (j + ns, 0)),
            pl.BlockSpec((L, rb), lambda b, j: (0, j)),
            pl.BlockSpec((L, rb), lambda b, j: (0, j + ns)),
            pl.BlockSpec((rb, 2 * C_WIDTH), lambda b, j: (j, 0)),
            pl.BlockSpec((rb, 2 * C_WIDTH), lambda b, j: (j + ns, 0)),
        ],
        out_specs=pl.BlockSpec((L, C_WIDTH), lambda b, j: (b, 0)),
        out_shape=jax.ShapeDtypeStruct((batch * L, C_WIDTH), F32),
        compiler_params=_cparams("parallel", "arbitrary"),
        name="hyena_conv",
    )(ub, w, w, w2, w2, a_spec, a_spec)


def _proj_cd_kernel(x0_ref, y_ref, u_ref, skip_ref, yd_ref, w_ref, x_ref, g1_ref, o_ref):
    yh = (x0_ref[...] * (y_ref[...] + u_ref[...] * skip_ref[...])).astype(BF16)
    acc = jnp.dot(yh, w_ref[:C_WIDTH, :], preferred_element_type=F32)
    acc += jnp.dot(yd_ref[...].astype(BF16), w_ref[C_WIDTH:, :], preferred_element_type=F32)
    o_ref[...] = x_ref[...] + g1_ref[0] * acc


def _proj_cd(x0, y, u, skip, yd, w_out, x, g1, row_of, tm):
    n, d = x.shape
    half = pl.BlockSpec((tm, C_WIDTH), lambda i: (i, 0))
    return pl.pallas_call(
        _proj_cd_kernel,
        grid=(n // tm,),
        in_specs=[
            half, half, half,
            pl.BlockSpec((1, C_WIDTH), lambda i: (0, 0)),
            half,
            pl.BlockSpec(w_out.shape, lambda i: (0, 0)),
            pl.BlockSpec((tm, d), lambda i: (i, 0)),
            pl.BlockSpec((1, 1, d), lambda i: (row_of(i), 0, 0)),
        ],
        out_specs=pl.BlockSpec((tm, d), lambda i: (i, 0)),
        out_shape=jax.ShapeDtypeStruct((n, d), F32),
        compiler_params=_cparams("parallel"),
        name="proj_cd",
    )(x0, y, u, skip, yd, w_out, x, g1)


def _dft_mats(L):
    m = 2 * L
    r = jnp.arange(L, dtype=jnp.int32)
    ks = (r[:, None] * r[None, :]) % m
    ang = ks.astype(F32) * (2.0 * math.pi / m)
    c = jnp.cos(ang)
    s = jnp.sin(ang)
    alt = jnp.where(r % 2 == 0, 1.0, -1.0).astype(F32)
    is0 = (r == 0)[:, None]
    fwd = jnp.concatenate([c, jnp.where(is0, alt[None, :], s)], axis=0)
    inv_top = jnp.where(is0, 1.0 / m, c * (2.0 / m))
    inv_bot = jnp.where(is0, alt[None, :] / m, s * (-2.0 / m))
    inv = jnp.concatenate([inv_top, inv_bot], axis=0).T
    return fwd.astype(BF16), inv.astype(BF16)


def _peer_kernel(x_ref, sc_ref, sh_ref, g2_ref, ng_ref, wqT_ref, keys_ref, u_ref, vT_ref, o_ref,
                 h2T_s, sT_s, tops_s, th_s, iz_s, aT_s, bT_s, thT_s, act_s, w_s, acc_s):
    e = pl.program_id(1)
    ne = pl.num_programs(1)
    nk = keys_ref.shape[1]
    T = x_ref.shape[0]
    eb = u_ref.shape[0]
    nh = PEER_HEADS
    ntc = T // LANES
    neg = -jnp.inf

    @pl.when(e == 0)
    def _route():
        h2 = _norm_mod(x_ref[...], ng_ref[...], sc_ref[0], sh_ref[0])
        h2T = h2.T.astype(BF16)
        h2T_s[...] = h2T
        qT = jnp.dot(wqT_ref[...], h2T, preferred_element_type=F32).astype(BF16)
        pk = keys_ref.shape[2]
        for hp in range(2 * nh):
            sT_s[hp] = jnp.dot(keys_ref[hp], qT[hp * pk:(hp + 1) * pk, :], preferred_element_type=F32)

        def top_body(hp, carry):
            s = sT_s[hp]
            for i in range(PEER_TOPK):
                m = jnp.max(s, axis=0, keepdims=True)
                tops_s[hp, i:i + 1, :] = m
                s = jnp.where(s >= m, neg, s)
            return carry

        lax.fori_loop(0, 2 * nh, top_body, 0)

        pairs = [(i, j) for i in range(PEER_TOPK) for j in range(PEER_TOPK // (i + 1))]
        for tc in range(ntc):
            sl = slice(tc * LANES, (tc + 1) * LANES)
            A = [jnp.concatenate([tops_s[2 * h, i:i + 1, sl] for h in range(nh)], axis=0) for i in range(PEER_TOPK)]
            B = [jnp.concatenate([tops_s[2 * h + 1, j:j + 1, sl] for h in range(nh)], axis=0)
                 for j in range(PEER_TOPK)]
            cand = [A[i] + B[j] for (i, j) in pairs]
            work = list(cand)
            tau = None
            for r in range(PEER_TOPK):
                tau = functools.reduce(jnp.maximum, work)
                if r < PEER_TOPK - 1:
                    work = [jnp.where(c >= tau, neg, c) for c in work]
            m0 = cand[0]
            z = functools.reduce(lambda a, b: a + b, [jnp.where(c >= tau, jnp.exp(c - m0), 0.0) for c in cand])
            iz = 1.0 / z
            for h in range(nh):
                iz_s[h, 0:1, sl] = iz[h:h + 1]
            for i in range(PEER_TOPK):
                row = [jnp.where(cand[n] >= tau, B[j], jnp.inf) for n, (ii, j) in enumerate(pairs) if ii == i]
                cut = functools.reduce(jnp.minimum, row)
                for h in range(nh):
                    th_s[h, i:i + 1, sl] = cut[h:h + 1]

        def head_body(h, carry):
            s0 = sT_s[2 * h]
            s1 = sT_s[2 * h + 1]
            a0 = tops_s[2 * h, 0:1, :]
            b0 = tops_s[2 * h + 1, 0:1, :]
            aT_s[h] = jnp.exp(s0 - a0) * iz_s[h, 0:1, :]
            bT_s[h] = jnp.exp(s1 - b0)
            th = jnp.full(s0.shape, jnp.inf, F32)
            for i in range(PEER_TOPK):
                th = jnp.where(s0 == tops_s[2 * h, i:i + 1, :], th_s[h, i:i + 1, :], th)
            thT_s[h] = th
            return carry

        lax.fori_loop(0, nh, head_body, 0)

    act_s[...] = jnp.dot(u_ref[...], h2T_s[...], preferred_element_type=F32)

    for grp in range(eb // nk // SUBLANES):
        base = pl.multiple_of(e * (eb // nk) + grp * SUBLANES, SUBLANES)
        for tc in range(ntc):
            sl = slice(tc * LANES, (tc + 1) * LANES)
            th8 = [thT_s[h, pl.ds(base, SUBLANES), sl] for h in range(nh)]
            a8 = [aT_s[h, pl.ds(base, SUBLANES), sl] for h in range(nh)]
            for ii in range(SUBLANES):
                g = jnp.zeros((nk, LANES), F32)
                for h in range(nh):
                    g = g + jnp.where(sT_s[2 * h + 1, :, sl] >= th8[h][ii:ii + 1], a8[h][ii:ii + 1] * bT_s[h, :, sl], 0.0)
                rows = slice((grp * SUBLANES + ii) * nk, (grp * SUBLANES + ii + 1) * nk)
                w_s[rows, sl] = (g * _gelu(act_s[rows, sl])).astype(BF16)
    contrib = jnp.dot(vT_ref[...], w_s[...], preferred_element_type=F32)

    @pl.when(e == 0)
    def _():
        acc_s[...] = contrib

    @pl.when(e > 0)
    def _():
        acc_s[...] += contrib

    @pl.when(e == ne - 1)
    def _():
        o_ref[...] = x_ref[...] + g2_ref[0] * acc_s[...].T


def _peer(x, sc, sh, g2, ng, wqT, keys, u_b, vT_b, row_of, T, eb):
    n, d = x.shape
    n_exp = u_b.shape[0]
    nk = keys.shape[1]
    nh = PEER_HEADS
    modspec = pl.BlockSpec((1, 1, d), lambda i, e: (row_of(i), 0, 0))
    return pl.pallas_call(
        _peer_kernel,
        grid=(n // T, n_exp // eb),
        in_specs=[
            pl.BlockSpec((T, d), lambda i, e: (i, 0)),
            modspec, modspec, modspec,
            pl.BlockSpec((1, d), lambda i, e: (0, 0)),
            pl.BlockSpec(wqT.shape, lambda i, e: (0, 0)),
            pl.BlockSpec(keys.shape, lambda i, e: (0, 0, 0)),
            pl.BlockSpec((eb, d), lambda i, e: (e, 0)),
            pl.BlockSpec((d, eb), lambda i, e: (0, e)),
        ],
        out_specs=pl.BlockSpec((T, d), lambda i, e: (i, 0)),
        out_shape=jax.ShapeDtypeStruct((n, d), F32),
        scratch_shapes=[
            pltpu.VMEM((d, T), BF16),
            pltpu.VMEM((2 * nh, nk, T), F32),
            pltpu.VMEM((2 * nh, PEER_TOPK, T), F32),
            pltpu.VMEM((nh, PEER_TOPK, T), F32),
            pltpu.VMEM((nh, SUBLANES, T), F32),
            pltpu.VMEM((nh, nk, T), F32),
            pltpu.VMEM((nh, nk, T), F32),
            pltpu.VMEM((nh, nk, T), F32),
            pltpu.VMEM((eb, T), F32),
            pltpu.VMEM((eb, T), BF16),
            pltpu.VMEM((d, T), F32),
        ],
        compiler_params=_cparams("parallel", "arbitrary"),
        name="peer",
    )(x, sc, sh, g2, ng, wqT, keys, u_b, vT_b)


def _pick(n, pref):
    t = pref
    while n % t:
        t //= 2
    return t


def kernel(x, c, ctx, c_ctx, w_mod, b_mod, norm1_g, norm2_g, ab_w_in, a_ws, a_bs, a_vnorm_g, b_conv_w, ab_w_out,
           cd_w_in, c_short_w, c_short_b, c_filt_w1, c_filt_b1, c_filt_w2, c_filt_b2, c_filt_w3, c_filt_b3,
           c_filt_wout, c_filt_freq, c_skip, d_q_norm_g, d_k_norm_g, d_lambda_q1, d_lambda_k1, d_lambda_q2,
           d_lambda_k2, d_subln_g, cd_w_out, peer_wq, peer_keys, peer_u, peer_v):
    B, L, D = x.shape
    Lc = ctx.shape[1]
    depth = w_mod.shape[0]
    n_lat, n_ctx = B * L, B * Lc

    tm_l, tm_c = _pick(L, 256), _pick(Lc, 256)
    tq_l, tq_c = _pick(L, 256), _pick(Lc, 256)
    tp_l, tp_c = _pick(L, 512), _pick(n_ctx, 512)
    eb = 1024

    mod_rows = 2 * SUBLANES
    s_rows = jnp.zeros((mod_rows, D), F32).at[:B].set(c).at[B].set(c_ctx)
    mod = _mod_all(s_rows, w_mod, b_mod).reshape(depth, mod_rows, N_MOD, 1, D)

    xl = x.reshape(n_lat, D)
    xc = ctx.reshape(n_ctx, D)

    def lat_row(t):
        return lambda i: (i * t) // L

    ctx_row = lambda i: B

    col = jnp.arange(QK_W)
    inv = ROPE_BASE ** (-jnp.arange(0, ROPE_AXIS_DIM, 2, dtype=F32) / ROPE_AXIS_DIM)
    pos = jnp.arange(L)
    rows_f = (pos // GRID_W).astype(F32)
    cols_f = (pos % GRID_W).astype(F32)
    is_row_part = (col % D_QK) < ROPE_AXIS_DIM
    ang = jnp.where(is_row_part[None, :], rows_f[:, None], cols_f[:, None]) * inv[col % (ROPE_AXIS_DIM // 2)][None, :]
    cos_l = jnp.cos(ang)
    sin_l = jnp.where(((col % ROPE_AXIS_DIM) < ROPE_AXIS_DIM // 2)[None, :], -jnp.sin(ang), jnp.sin(ang))
    cos_c = jnp.ones((tm_c, QK_W), F32)
    sin_c = jnp.zeros((tm_c, QK_W), F32)

    deltas = jnp.abs(jnp.linspace(math.log(HY_TARGET) / HY_DECAY_LONG_PCT, math.log(HY_TARGET) / HY_DECAY_SHORT_PCT,
                                  C_WIDTH, dtype=F32))[None, :]

    for l in range(depth):
        last = l == depth - 1
        odd = l % 2 == 1
        i = l // 2
        m = lambda k: mod[l, :, k]
        sh1, sc1, g1, sh2, sc2, g2 = (m(k) for k in range(N_MOD))
        n1 = norm1_g[l][None, :]
        need_ctx = (not last) or odd

        if not odd:
            w_in = ab_w_in[i].astype(BF16)
            w_out = ab_w_out[i].astype(BF16)
            ws = a_ws[i].astype(BF16)
            bias_full = jnp.repeat(a_bs[i].T, A_HEAD_DIM, axis=1)
            vg = a_vnorm_g[i][None, :]
            cw = b_conv_w[i]
            p = _norm_mod_matmul(xl, sc1, sh1, n1, w_in, lat_row(tm_l), tm_l)
            xl = _mix_ab(p, xl, g1, ws, bias_full, vg, cw, w_out, lat_row(tm_l), L, tm_l)
            if not last:
                pc = _norm_mod_matmul(xc, sc1, sh1, n1, w_in, ctx_row, tm_c)
                xc = _mix_ab(pc, xc, g1, ws, bias_full, vg, cw, w_out, ctx_row, Lc, tm_c)
        else:
            lam_init = 0.8 - 0.6 * math.exp(-0.3 * l)
            w_in = cd_w_in[i].astype(BF16)
            w_out = cd_w_out[i].astype(BF16)
            qg = jnp.tile(d_q_norm_g[i], QK_W // D_QK)[None, :]
            kg = jnp.tile(d_k_norm_g[i], QK_W // D_QK)[None, :]
            lam_vecs = tuple(v[i][None, :] for v in (d_lambda_q1, d_lambda_k1, d_lambda_q2, d_lambda_k2))
            sg = d_subln_g[i][None, :]
            fw = LANES
            padc = lambda a: jnp.pad(a, ((0, 0), (0, fw - a.shape[1])))
            padr = lambda a: jnp.pad(a, ((0, fw - a.shape[0]), (0, 0)))
            fp = (padc(padr(c_filt_w1[i])), padc(c_filt_b1[i][None, :]), padc(padr(c_filt_w2[i])),
                  padc(c_filt_b2[i][None, :]), padc(padr(c_filt_w3[i])), padc(c_filt_b3[i][None, :]),
                  padr(c_filt_wout[i]), padc(c_filt_freq[i][None, :]))
            sw, sb, skip = c_short_w[i], c_short_b[i][None, :], c_skip[i][None, :]

            def hyena(pp, batch, seq, tm):
                x0, u, ub = _hy_pre(pp, sw, sb, seq, tm)
                fwd, invm = _dft_mats(seq)
                filt = _hy_filter(seq, fp, deltas)
                spec = _matmul(fwd, filt, _pick(2 * seq, 512))
                y = _hy_conv(ub, fwd, invm, spec, batch, seq, _pick(seq, 256))
                return x0, y, u

            pc = _norm_mod_matmul(xc, sc1, sh1, n1, w_in, ctx_row, tm_c)
            qmc, kc = _qk_prep(pc, cos_c, sin_c, qg, kg, lambda t: 0, tm_c)
            vc = pc[:, HY_IN + 2 * QK_W:].astype(BF16)
            p = _norm_mod_matmul(xl, sc1, sh1, n1, w_in, lat_row(tm_l), tm_l)
            qm, k = _qk_prep(p, cos_l, sin_l, qg, kg, lambda t: t % (L // tm_l), tm_l)
            v = p[:, HY_IN + 2 * QK_W:].astype(BF16)
            k_all = jnp.concatenate([kc.reshape(B, Lc, QK_W), k.reshape(B, L, QK_W)], axis=1)
            v_all = jnp.concatenate([vc.reshape(B, Lc, V_W), v.reshape(B, L, V_W)], axis=1)
            yd = _diff_attn(qm, k_all, v_all, lam_vecs, sg, lam_init, B, L, tq_l)
            x0, y, u = hyena(p, B, L, tm_l)
            xl = _proj_cd(x0, y, u, skip, yd, w_out, xl, g1, lat_row(tm_l), tm_l)
            if not last:
                ydc = _diff_attn(qmc, kc.reshape(B, Lc, QK_W), vc.reshape(B, Lc, V_W), lam_vecs, sg, lam_init,
                                 B, Lc, tq_c)
                x0c, yc, uc = hyena(pc, B, Lc, tm_c)
                xc = _proj_cd(x0c, yc, uc, skip, ydc, w_out, xc, g1, ctx_row, tm_c)

        n2 = norm2_g[l][None, :]
        wqT = peer_wq[l].T.astype(BF16)
        keys = peer_keys[l].reshape(2 * PEER_HEADS, peer_keys.shape[3], peer_keys.shape[4]).astype(BF16)
        u_b = peer_u[l].astype(BF16)
        vT_b = peer_v[l].T.astype(BF16)
        xl = _peer(xl, sc2, sh2, g2, n2, wqT, keys, u_b, vT_b, lat_row(tp_l), tp_l, eb)
        if not last:
            xc = _peer(xc, sc2, sh2, g2, n2, wqT, keys, u_b, vT_b, ctx_row, tp_c, eb)

    return xl.reshape(B, L, D)
```

```python
import functools
import math

import jax
import jax.numpy as jnp
from jax import lax
from jax.experimental import pallas as pl
from jax.experimental.pallas import tpu as pltpu

F32 = jnp.float32
BF16 = jnp.bfloat16
HI = lax.Precision.HIGHEST

GRID_W = 64
N_MOD = 6
RMS_EPS = 1e-6
CHUNK = 128
A_HEADS = 8
A_HEAD_DIM = 64
A_WIDTH = A_HEADS * A_HEAD_DIM
B_WIDTH = 512
C_WIDTH = 512
HY_IN = 3 * C_WIDTH
HY_EMB = 33
HY_DECAY_LONG_PCT = 1.5
HY_DECAY_SHORT_PCT = 0.3
HY_TARGET = 1e-2
D_HEADS = 4
D_QK = 64
D_V = 2 * D_QK
QK_W = D_HEADS * 2 * D_QK
V_W = D_HEADS * D_V
ATTN_SCALE = D_QK ** -0.5
ROPE_AXIS_DIM = D_QK // 2
ROPE_BASE = 10000.0
PEER_HEADS = 8
PEER_TOPK = 16

LANES = 128
SUBLANES = 8
VMEM_LIMIT = 56 * 1024 * 1024
GATE_ROWS = 32


def _cparams(*sem):
    return pltpu.CompilerParams(dimension_semantics=sem, vmem_limit_bytes=VMEM_LIMIT)


def _gelu(x):
    return 0.5 * x * (1.0 + lax.erf(x * (2.0 ** -0.5)))


def _group_mean_sq(v, group):
    w = v.shape[-1]
    r = lax.broadcasted_iota(jnp.int32, (w, w), 0) // group
    c = lax.broadcasted_iota(jnp.int32, (w, w), 1) // group
    bd = jnp.where(r == c, 1.0, 0.0).astype(BF16)
    vv = v * v
    hi = vv.astype(BF16)
    lo = (vv - hi.astype(F32)).astype(BF16)
    s = jnp.dot(hi, bd, preferred_element_type=F32) + jnp.dot(lo, bd, preferred_element_type=F32)
    return s * (1.0 / group)


def _mod_kernel(s_ref, w_ref, b_ref, o_ref):
    s = s_ref[...]
    s = s / (1.0 + jnp.exp(-s))
    o_ref[0] = jnp.dot(s, w_ref[0], precision=HI, preferred_element_type=F32) + b_ref[0]


def _mod_all(s_rows, w_mod, b_mod):
    depth, d, n6 = w_mod.shape
    rows = s_rows.shape[0]
    tn = n6 // 4
    return pl.pallas_call(
        _mod_kernel,
        grid=(depth, n6 // tn),
        in_specs=[
            pl.BlockSpec((rows, d), lambda l, j: (0, 0)),
            pl.BlockSpec((1, d, tn), lambda l, j: (l, 0, j)),
            pl.BlockSpec((1, 1, tn), lambda l, j: (l, 0, j)),
        ],
        out_specs=pl.BlockSpec((1, rows, tn), lambda l, j: (l, 0, j)),
        out_shape=jax.ShapeDtypeStruct((depth, rows, n6), F32),
        compiler_params=_cparams("arbitrary", "arbitrary"),
        name="mod",
    )(s_rows, w_mod, b_mod.reshape(depth, 1, n6))


def _norm_mod(x, g, sc, sh):
    ms = jnp.mean(x * x, axis=-1, keepdims=True)
    return (x * lax.rsqrt(ms + RMS_EPS) * g) * (1.0 + sc) + sh


def _nmm_kernel(x_ref, sc_ref, sh_ref, g_ref, w_ref, o_ref):
    h = _norm_mod(x_ref[...], g_ref[...], sc_ref[0], sh_ref[0])
    o_ref[...] = jnp.dot(h.astype(BF16), w_ref[...], preferred_element_type=F32)


def _norm_mod_matmul(x, sc, sh, g, w, row_of, tm):
    n, d = x.shape
    nout = w.shape[1]
    return pl.pallas_call(
        _nmm_kernel,
        grid=(n // tm,),
        in_specs=[
            pl.BlockSpec((tm, d), lambda i: (i, 0)),
            pl.BlockSpec((1, 1, d), lambda i: (row_of(i), 0, 0)),
            pl.BlockSpec((1, 1, d), lambda i: (row_of(i), 0, 0)),
            pl.BlockSpec((1, d), lambda i: (0, 0)),
            pl.BlockSpec((d, nout), lambda i: (0, 0)),
        ],
        out_specs=pl.BlockSpec((tm, nout), lambda i: (i, 0)),
        out_shape=jax.ShapeDtypeStruct((n, nout), F32),
        compiler_params=_cparams("parallel"),
        name="norm_mod_inproj",
    )(x, sc, sh, g, w)


def _shift_rows(z, prev_row, next_row):
    tm = z.shape[0]
    row = lax.broadcasted_iota(jnp.int32, z.shape, 0)
    up = jnp.where(row == 0, prev_row, pltpu.roll(z, 1, 0))
    dn = jnp.where(row == tm - 1, next_row, pltpu.roll(z, tm - 1, 0))
    return up, dn


def _mix_ab_kernel(p_ref, hbp_ref, gcp_ref, hbn_ref, gcn_ref, x_ref, g1_ref, ws_ref, bias_ref, vg_ref, cw_ref,
                   wout_ref, o_ref, *, tiles_per_seq):
    i = pl.program_id(0)
    tm = p_ref.shape[0]
    aw, bw = A_WIDTH, B_WIDTH
    u = _gelu(p_ref[:, 0:aw])
    v = _gelu(p_ref[:, aw:2 * aw])
    vn = (v * lax.rsqrt(_group_mean_sq(v, A_HEAD_DIM) + RMS_EPS) * vg_ref[...]).astype(BF16)
    sv_chunks = []
    for cidx in range(tm // CHUNK):
        vc = vn[cidx * CHUNK:(cidx + 1) * CHUNK]
        heads = [jnp.dot(ws_ref[h], vc[:, h * A_HEAD_DIM:(h + 1) * A_HEAD_DIM], preferred_element_type=F32)
                 for h in range(A_HEADS)]
        sv_chunks.append(jnp.concatenate(heads, axis=-1) + bias_ref[...])
    sv = jnp.concatenate(sv_chunks, axis=0) if len(sv_chunks) > 1 else sv_chunks[0]
    ya = u * sv

    o0 = 2 * aw
    hb = p_ref[:, o0:o0 + bw]
    gb = p_ref[:, o0 + bw:o0 + 2 * bw]
    gc = p_ref[:, o0 + 2 * bw:o0 + 3 * bw]
    z = gc * hb
    pos = i % tiles_per_seq
    zp = jnp.where(pos == 0, 0.0, gcp_ref[SUBLANES - 1:SUBLANES, :] * hbp_ref[SUBLANES - 1:SUBLANES, :])
    zn = jnp.where(pos == tiles_per_seq - 1, 0.0, gcn_ref[0:1, :] * hbn_ref[0:1, :])
    z_up, z_dn = _shift_rows(z, zp, zn)
    yb = gb * (cw_ref[0:1, :] * z_up + cw_ref[1:2, :] * z + cw_ref[2:3, :] * z_dn)

    y = jnp.concatenate([ya, yb], axis=-1).astype(BF16)
    o_ref[...] = x_ref[...] + g1_ref[0] * jnp.dot(y, wout_ref[...], preferred_element_type=F32)


def _mix_ab(p, x, g1, ws, bias_full, vg, cw, w_out, row_of, seq_len, tm):
    n, d = x.shape
    nin = p.shape[1]
    nb8 = n // SUBLANES
    r8 = tm // SUBLANES
    hb_blk = (2 * A_WIDTH) // B_WIDTH
    gc_blk = hb_blk + 2
    prev = lambda i: jnp.maximum(i * r8 - 1, 0)
    nxt = lambda i: jnp.minimum((i + 1) * r8, nb8 - 1)
    full = lambda a: pl.BlockSpec(a.shape, lambda i: (0,) * a.ndim)
    return pl.pallas_call(
        functools.partial(_mix_ab_kernel, tiles_per_seq=seq_len // tm),
        grid=(n // tm,),
        in_specs=[
            pl.BlockSpec((tm, nin), lambda i: (i, 0)),
            pl.BlockSpec((SUBLANES, B_WIDTH), lambda i: (prev(i), hb_blk)),
            pl.BlockSpec((SUBLANES, B_WIDTH), lambda i: (prev(i), gc_blk)),
            pl.BlockSpec((SUBLANES, B_WIDTH), lambda i: (nxt(i), hb_blk)),
            pl.BlockSpec((SUBLANES, B_WIDTH), lambda i: (nxt(i), gc_blk)),
            pl.BlockSpec((tm, d), lambda i: (i, 0)),
            pl.BlockSpec((1, 1, d), lambda i: (row_of(i), 0, 0)),
            full(ws), full(bias_full), full(vg), full(cw), full(w_out),
        ],
        out_specs=pl.BlockSpec((tm, d), lambda i: (i, 0)),
        out_shape=jax.ShapeDtypeStruct((n, d), F32),
        compiler_params=_cparams("parallel"),
        name="mix_ab",
    )(p, p, p, p, p, x, g1, ws, bias_full, vg, cw, w_out)


def _qk_prep_kernel(q_ref, k_ref, cos_ref, sin_ref, qg_ref, kg_ref, qm_ref, ko_ref):
    lane = lax.broadcasted_iota(jnp.int32, q_ref.shape, 1)
    first = (lane % ROPE_AXIS_DIM) < (ROPE_AXIS_DIM // 2)
    w = q_ref.shape[1]
    half = ROPE_AXIS_DIM // 2

    def prep(t, g):
        t = t * lax.rsqrt(_group_mean_sq(t, D_QK) + RMS_EPS) * g
        partner = jnp.where(first, pltpu.roll(t, w - half, 1), pltpu.roll(t, half, 1))
        return t * cos_ref[...] + partner * sin_ref[...]

    q = prep(q_ref[...], qg_ref[...]) * ATTN_SCALE
    in_h0 = (lane % (2 * D_QK)) < D_QK
    qm_ref[0] = jnp.where(in_h0, q, 0.0).astype(BF16)
    qm_ref[1] = jnp.where(in_h0, 0.0, q).astype(BF16)
    ko_ref[...] = prep(k_ref[...], kg_ref[...]).astype(BF16)


def _qk_prep(p, cos_t, sin_t, qg, kg, tab_of, tm):
    n = p.shape[0]
    q_blk = HY_IN // QK_W
    return pl.pallas_call(
        _qk_prep_kernel,
        grid=(n // tm,),
        in_specs=[
            pl.BlockSpec((tm, QK_W), lambda i: (i, q_blk)),
            pl.BlockSpec((tm, QK_W), lambda i: (i, q_blk + 1)),
            pl.BlockSpec((tm, QK_W), lambda i: (tab_of(i), 0)),
            pl.BlockSpec((tm, QK_W), lambda i: (tab_of(i), 0)),
            pl.BlockSpec((1, QK_W), lambda i: (0, 0)),
            pl.BlockSpec((1, QK_W), lambda i: (0, 0)),
        ],
        out_specs=[
            pl.BlockSpec((2, tm, QK_W), lambda i: (0, i, 0)),
            pl.BlockSpec((tm, QK_W), lambda i: (i, 0)),
        ],
        out_shape=[jax.ShapeDtypeStruct((2, n, QK_W), BF16), jax.ShapeDtypeStruct((n, QK_W), BF16)],
        compiler_params=_cparams("parallel"),
        name="qk_prep",
    )(p, p, cos_t, sin_t, qg, kg)


def _diff_attn_kernel(q0_ref, q1_ref, k_ref, v_ref, lq1_ref, lk1_ref, lq2_ref, lk2_ref, sg_ref, o_ref, *, lam_init):
    lam = (jnp.exp(jnp.sum(lq1_ref[...] * lk1_ref[...], axis=-1, keepdims=True))
           - jnp.exp(jnp.sum(lq2_ref[...] * lk2_ref[...], axis=-1, keepdims=True)) + lam_init)
    k = k_ref[0]
    v = v_ref[0]

    def half(q):
        s = lax.dot_general(q, k, (((1,), (1,)), ((), ())), preferred_element_type=F32)
        m = jnp.max(s, axis=-1, keepdims=True)
        p = jnp.exp(s - m)
        l = jnp.sum(p, axis=-1, keepdims=True)
        return jnp.dot(p.astype(BF16), v, preferred_element_type=F32) / l

    o = half(q0_ref[0]) - lam * half(q1_ref[0])
    ms = jnp.mean(o * o, axis=-1, keepdims=True)
    o_ref[...] = (o * lax.rsqrt(ms + RMS_EPS) * sg_ref[...]) * (1.0 - lam_init)


def _diff_attn(qm, k_all, v_all, lam_vecs, subln_g, lam_init, batch, lq, tq):
    lk = k_all.shape[1]
    nq = lq // tq
    vec = lambda a: pl.BlockSpec(a.shape, lambda b, h, i: (0, 0))
    return pl.pallas_call(
        functools.partial(_diff_attn_kernel, lam_init=lam_init),
        grid=(batch, D_HEADS, nq),
        in_specs=[
            pl.BlockSpec((1, tq, D_V), lambda b, h, i: (0, b * nq + i, h)),
            pl.BlockSpec((1, tq, D_V), lambda b, h, i: (1, b * nq + i, h)),
            pl.BlockSpec((1, lk, D_V), lambda b, h, i: (b, 0, h)),
            pl.BlockSpec((1, lk, D_V), lambda b, h, i: (b, 0, h)),
            vec(lam_vecs[0]), vec(lam_vecs[1]), vec(lam_vecs[2]), vec(lam_vecs[3]), vec(subln_g),
        ],
        out_specs=pl.BlockSpec((tq, D_V), lambda b, h, i: (b * nq + i, h)),
        out_shape=jax.ShapeDtypeStruct((batch * lq, V_W), F32),
        compiler_params=_cparams("parallel", "parallel", "arbitrary"),
        name="diff_attn",
    )(qm, qm, k_all, v_all, *lam_vecs, subln_g)


def _hy_pre_kernel(p_ref, pp_ref, pn_ref, sw_ref, sb_ref, x0_ref, u_ref, ub_ref, *, tiles_per_seq):
    i = pl.program_id(0)
    pos = i % tiles_per_seq
    z = p_ref[...]
    zp = jnp.where(pos == 0, 0.0, pp_ref[SUBLANES - 1:SUBLANES, :])
    zn = jnp.where(pos == tiles_per_seq - 1, 0.0, pn_ref[0:1, :])
    z_up, z_dn = _shift_rows(z, zp, zn)
    c = sw_ref[0:1, :] * z_up + sw_ref[1:2, :] * z + sw_ref[2:3, :] * z_dn + sb_ref[...]
    cw = C_WIDTH
    u = c[:, cw:2 * cw] * c[:, 2 * cw:3 * cw]
    x0_ref[...] = c[:, 0:cw]
    u_ref[...] = u
    ub_ref[...] = u.astype(BF16)


def _hy_pre(p, sw, sb, seq_len, tm):
    n = p.shape[0]
    nb8 = n // SUBLANES
    r8 = tm // SUBLANES
    prev = lambda i: jnp.maximum(i * r8 - 1, 0)
    nxt = lambda i: jnp.minimum((i + 1) * r8, nb8 - 1)
    o_spec = pl.BlockSpec((tm, C_WIDTH), lambda i: (i, 0))
    return pl.pallas_call(
        functools.partial(_hy_pre_kernel, tiles_per_seq=seq_len // tm),
        grid=(n // tm,),
        in_specs=[
            pl.BlockSpec((tm, HY_IN), lambda i: (i, 0)),
            pl.BlockSpec((SUBLANES, HY_IN), lambda i: (prev(i), 0)),
            pl.BlockSpec((SUBLANES, HY_IN), lambda i: (nxt(i), 0)),
            pl.BlockSpec((3, HY_IN), lambda i: (0, 0)),
            pl.BlockSpec((1, HY_IN), lambda i: (0, 0)),
        ],
        out_specs=[o_spec, o_spec, o_spec],
        out_shape=[jax.ShapeDtypeStruct((n, C_WIDTH), F32), jax.ShapeDtypeStruct((n, C_WIDTH), F32),
                   jax.ShapeDtypeStruct((n, C_WIDTH), BF16)],
        compiler_params=_cparams("parallel"),
        name="hyena_pre",
    )(p, p, p, sw, sb)


def _hy_filter_kernel(w1_ref, b1_ref, w2_ref, b2_ref, w3_ref, b3_ref, wo_ref, fr_ref, dl_ref, o_ref):
    L = o_ref.shape[0]
    fw = w1_ref.shape[0]
    bands = (HY_EMB - 1) // 2
    row = lax.broadcasted_iota(jnp.int32, (L, fw), 0).astype(F32)
    col = lax.broadcasted_iota(jnp.int32, (L, fw), 1)
    t = row / (L - 1.0)
    ang = (2.0 * math.pi) * row / L
    step = (bands - 1 - 1e-4) / (bands - 1)
    band = jnp.where(col <= bands, col - 1, col - 1 - bands).astype(F32)
    arg = (1e-4 + band * step) * ang
    feats = jnp.where(col == 0, t, jnp.where(col <= bands, jnp.cos(arg), jnp.where(col < HY_EMB, -jnp.sin(arg), 0.0)))
    fr = fr_ref[...]
    mm = lambda a, w: jnp.dot(a, w, precision=HI, preferred_element_type=F32)
    a = jnp.sin(fr * (mm(feats, w1_ref[...]) + b1_ref[...]))
    a = jnp.sin(fr * (mm(a, w2_ref[...]) + b2_ref[...]))
    a = jnp.sin(fr * (mm(a, w3_ref[...]) + b3_ref[...]))
    hf = mm(a, wo_ref[...])
    rowc = lax.broadcasted_iota(jnp.int32, (L, C_WIDTH), 0)
    tc = rowc.astype(F32) / (L - 1.0)
    decay = jnp.exp(-tc * dl_ref[...])
    h_fwd = hf[:, :C_WIDTH] * decay
    h_bwd = jnp.where(rowc == 0, 0.0, hf[:, C_WIDTH:] * decay)
    norm = jnp.sum(jnp.abs(h_fwd) + jnp.abs(h_bwd), axis=0, keepdims=True)
    o_ref[:, :C_WIDTH] = (h_fwd / norm).astype(BF16)
    o_ref[:, C_WIDTH:] = (h_bwd / norm).astype(BF16)


def _hy_filter(L, fp, deltas):
    full = lambda a: pl.BlockSpec(a.shape, lambda: (0,) * a.ndim)
    args = (*fp, deltas)
    return pl.pallas_call(
        _hy_filter_kernel,
        in_specs=[full(a) for a in args],
        out_specs=pl.BlockSpec((L, 2 * C_WIDTH), lambda: (0, 0)),
        out_shape=jax.ShapeDtypeStruct((L, 2 * C_WIDTH), BF16),
        compiler_params=pltpu.CompilerParams(vmem_limit_bytes=VMEM_LIMIT),
        name="hyena_filter",
    )(*args)


def _mm_kernel(a_ref, b_ref, o_ref):
    o_ref[...] = jnp.dot(a_ref[...], b_ref[...], preferred_element_type=F32)


def _matmul(a, b, tm):
    m, k = a.shape
    n = b.shape[1]
    return pl.pallas_call(
        _mm_kernel,
        grid=(m // tm,),
        in_specs=[pl.BlockSpec((tm, k), lambda i: (i, 0)), pl.BlockSpec((k, n), lambda i: (0, 0))],
        out_specs=pl.BlockSpec((tm, n), lambda i: (i, 0)),
        out_shape=jax.ShapeDtypeStruct((m, n), F32),
        compiler_params=_cparams("parallel"),
        name="dft_filter",
    )(a, b)


def _hy_conv_kernel(u_ref, wt_ref, wb_ref, w2t_ref, w2b_ref, at_ref, ab_ref, y_ref):
    j = pl.program_id(1)
    cw = C_WIDTH
    u = u_ref[...]
    uc = jnp.dot(wt_ref[...], u, preferred_element_type=F32)
    us = jnp.dot(wb_ref[...], u, preferred_element_type=F32)
    kre = at_ref[:, :cw] + at_ref[:, cw:]
    kim = ab_ref[:, cw:] - ab_ref[:, :cw]
    knyq = ab_ref[:, :cw] + ab_ref[:, cw:]
    row = lax.broadcasted_iota(jnp.int32, kre.shape, 0)
    special = jnp.logical_and(row == 0, j == 0)
    p2 = jnp.where(special, 0.0, kim)
    p4 = jnp.where(special, knyq, -kre)
    yt = (uc * kre + us * p2).astype(BF16)
    yb = (uc * p2 + us * p4).astype(BF16)
    L = y_ref.shape[0]
    tr = min(L, 512)

    def inverse_rows(rows, first):
        contrib = (jnp.dot(w2t_ref[rows, :], yt, preferred_element_type=F32)
                   + jnp.dot(w2b_ref[rows, :], yb, preferred_element_type=F32))
        y_ref[rows, :] = contrib if first else y_ref[rows, :] + contrib

    @pl.when(j == 0)
    def _():
        for c in range(L // tr):
            inverse_rows(slice(c * tr, (c + 1) * tr), True)

    @pl.when(j > 0)
    def _():
        for c in range(L // tr):
            inverse_rows(slice(c * tr, (c + 1) * tr), False)


def _hy_conv(ub, w, w2, a_spec, batch, L, rb):
    ns = L // rb
    return pl.pallas_call(
        _hy_conv_kernel,
        grid=(batch, ns),
        in_specs=[
            pl.BlockSpec((L, C_WIDTH), lambda b, j: (b, 0)),
            pl.BlockSpec((rb, L), lambda b, j: (j, 0)),
            pl.BlockSpec((rb, L), lambda b, j: (j + ns, 0)),
            pl.BlockSpec((L, rb), lambda b, j: (0, j)),
            pl.BlockSpec((L, rb), lambda b, j: (0, j + ns)),
            pl.BlockSpec((rb, 2 * C_WIDTH), lambda b, j: (j, 0)),
            pl.BlockSpec((rb, 2 * C_WIDTH), lambda b, j: (j + ns, 0)),
        ],
        out_specs=pl.BlockSpec((L, C_WIDTH), lambda b, j: (b, 0)),
        out_shape=jax.ShapeDtypeStruct((batch * L, C_WIDTH), F32),
        compiler_params=_cparams("parallel", "arbitrary"),
        name="hyena_conv",
    )(ub, w, w, w2, w2, a_spec, a_spec)


def _proj_cd_kernel(x0_ref, y_ref, u_ref, skip_ref, yd_ref, w_ref, x_ref, g1_ref, o_ref):
    yh = (x0_ref[...] * (y_ref[...] + u_ref[...] * skip_ref[...])).astype(BF16)
    acc = jnp.dot(yh, w_ref[:C_WIDTH, :], preferred_element_type=F32)
    acc += jnp.dot(yd_ref[...].astype(BF16), w_ref[C_WIDTH:, :], preferred_element_type=F32)
    o_ref[...] = x_ref[...] + g1_ref[0] * acc


def _proj_cd(x0, y, u, skip, yd, w_out, x, g1, row_of, tm):
    n, d = x.shape
    half = pl.BlockSpec((tm, C_WIDTH), lambda i: (i, 0))
    return pl.pallas_call(
        _proj_cd_kernel,
        grid=(n // tm,),
        in_specs=[
            half, half, half,
            pl.BlockSpec((1, C_WIDTH), lambda i: (0, 0)),
            half,
            pl.BlockSpec(w_out.shape, lambda i: (0, 0)),
            pl.BlockSpec((tm, d), lambda i: (i, 0)),
            pl.BlockSpec((1, 1, d), lambda i: (row_of(i), 0, 0)),
        ],
        out_specs=pl.BlockSpec((tm, d), lambda i: (i, 0)),
        out_shape=jax.ShapeDtypeStruct((n, d), F32),
        compiler_params=_cparams("parallel"),
        name="proj_cd",
    )(x0, y, u, skip, yd, w_out, x, g1)


def _dft_mats(L):
    m = 2 * L
    r = jnp.arange(L, dtype=jnp.int32)
    blk = math.gcd(L, 64)
    phase = lambda cols: ((r[:, None] * cols[None, :]) % m).astype(F32) * (2.0 * math.pi / m)
    a_hi = phase(jnp.arange(L // blk, dtype=jnp.int32) * blk)[:, :, None]
    a_lo = phase(jnp.arange(blk, dtype=jnp.int32))[:, None, :]
    c = (jnp.cos(a_hi) * jnp.cos(a_lo) - jnp.sin(a_hi) * jnp.sin(a_lo)).reshape(L, L)
    s = (jnp.sin(a_hi) * jnp.cos(a_lo) + jnp.cos(a_hi) * jnp.sin(a_lo)).reshape(L, L)
    alt = jnp.where(r % 2 == 0, 1.0, -1.0).astype(F32)
    is0 = (r == 0)[:, None]
    fwd = jnp.concatenate([c, jnp.where(is0, alt[None, :], s)], axis=0)
    inv_top = jnp.where(is0, 1.0 / m, c * (2.0 / m))
    inv_bot = jnp.where(is0, alt[None, :] / m, s * (-2.0 / m))
    inv = jnp.concatenate([inv_top, inv_bot], axis=0).T
    return fwd.astype(BF16), inv.astype(BF16)


def _peer_kernel(x_ref, sc_ref, sh_ref, g2_ref, ng_ref, wqT_ref, keys_ref, u0_ref, u_ref, vT_ref, o_ref,
                 h2T_s, sT_s, tops_s, th_s, iz_s, aT_s, bT_s, thT_s, act0_s, act1_s, w0_s, w1_s,
                 cutb0_s, cutb1_s, ab0_s, ab1_s, acc_s, *, nblk):
    e = pl.program_id(1)
    nk = keys_ref.shape[1]
    T = x_ref.shape[0]
    eb = u_ref.shape[0]
    nh = PEER_HEADS
    ntc = T // LANES
    neg = -jnp.inf

    def fill_rows(cutb, ab, blk):
        base = blk * SUBLANES if isinstance(blk, int) else pl.multiple_of(blk * SUBLANES, SUBLANES)
        for h in range(nh):
            cut8 = thT_s[h, pl.ds(base, SUBLANES), :]
            a8 = aT_s[h, pl.ds(base, SUBLANES), :]
            for ii in range(SUBLANES):
                cutb[h, ii] = jnp.broadcast_to(cut8[ii:ii + 1, :], (SUBLANES, T))
                ab[h, ii] = jnp.broadcast_to(a8[ii:ii + 1, :], (SUBLANES, T))

    @pl.when(e == 0)
    def _route():
        h2 = _norm_mod(x_ref[...], ng_ref[...], sc_ref[0], sh_ref[0])
        h2T = h2.T.astype(BF16)
        h2T_s[...] = h2T
        qT = jnp.dot(wqT_ref[...], h2T, preferred_element_type=F32).astype(BF16)
        pk = keys_ref.shape[2]
        for hp in range(2 * nh):
            sT_s[hp] = jnp.dot(keys_ref[hp], qT[hp * pk:(hp + 1) * pk, :], preferred_element_type=F32)

        def top_body(hp, carry):
            s = sT_s[hp]
            for i in range(PEER_TOPK):
                m = jnp.max(s, axis=0, keepdims=True)
                tops_s[hp, i:i + 1, :] = m
                s = jnp.where(s >= m, neg, s)
            return carry

        lax.fori_loop(0, 2 * nh, top_body, 0)

        pairs = [(i, j) for i in range(PEER_TOPK) for j in range(PEER_TOPK // (i + 1))]
        for tc in range(ntc):
            sl = slice(tc * LANES, (tc + 1) * LANES)
            A = [jnp.concatenate([tops_s[2 * h, i:i + 1, sl] for h in range(nh)], axis=0) for i in range(PEER_TOPK)]
            B = [jnp.concatenate([tops_s[2 * h + 1, j:j + 1, sl] for h in range(nh)], axis=0)
                 for j in range(PEER_TOPK)]
            cand = [A[i] + B[j] for (i, j) in pairs]
            work = list(cand)
            tau = None
            for r in range(PEER_TOPK):
                tau = functools.reduce(jnp.maximum, work)
                if r < PEER_TOPK - 1:
                    work = [jnp.where(c >= tau, neg, c) for c in work]
            m0 = cand[0]
            z = functools.reduce(lambda a, b: a + b, [jnp.where(c >= tau, jnp.exp(c - m0), 0.0) for c in cand])
            iz = 1.0 / z
            for h in range(nh):
                iz_s[h, 0:1, sl] = iz[h:h + 1]
            for i in range(PEER_TOPK):
                row = [jnp.where(cand[n] >= tau, B[j], jnp.inf) for n, (ii, j) in enumerate(pairs) if ii == i]
                cut = functools.reduce(jnp.minimum, row)
                for h in range(nh):
                    th_s[h, i:i + 1, sl] = cut[h:h + 1]

        def head_body(h, carry):
            s0 = sT_s[2 * h]
            s1 = sT_s[2 * h + 1]
            a0 = tops_s[2 * h, 0:1, :]
            b0 = tops_s[2 * h + 1, 0:1, :]
            aT_s[h] = jnp.exp(s0 - a0) * iz_s[h, 0:1, :]
            bT_s[h] = jnp.exp(s1 - b0)
            th = jnp.full(s0.shape, jnp.inf, F32)
            for i in range(PEER_TOPK):
                th = jnp.where(s0 == tops_s[2 * h, i:i + 1, :], th_s[h, i:i + 1, :], th)
            thT_s[h] = th
            return carry

        lax.fori_loop(0, nh, head_body, 0)

        act0_s[...] = jnp.dot(u0_ref[...], h2T, preferred_element_type=F32)
        w1_s[...] = jnp.zeros(w1_s.shape, BF16)
        acc_s[...] = jnp.zeros(acc_s.shape, F32)
        fill_rows(cutb0_s, ab0_s, 0)

    def stage(act_w, act_r, w_w, w_r, cutb_w, ab_w, cutb_r, ab_r):
        nrc = nk // GATE_ROWS
        nvr = GATE_ROWS // SUBLANES
        mrows = eb // ntc
        drows = acc_s.shape[0] // ntc
        for tc in range(ntc):
            sl = slice(tc * LANES, (tc + 1) * LANES)
            ru = slice(tc * mrows, (tc + 1) * mrows)
            act_w[ru, :] = jnp.dot(u_ref[ru, :], h2T_s[...], preferred_element_type=F32)
            rd = slice(tc * drows, (tc + 1) * drows)
            acc_s[rd, :] += jnp.dot(vT_ref[rd, :], w_r[...], preferred_element_type=F32)
            for rc in range(nrc):
                g = [[jnp.zeros((SUBLANES, LANES), F32) for _ in range(nvr)] for _ in range(SUBLANES)]
                for h in range(nh):
                    s1c = [sT_s[2 * h + 1, rc * GATE_ROWS + v * SUBLANES:rc * GATE_ROWS + (v + 1) * SUBLANES, sl]
                           for v in range(nvr)]
                    bc = [bT_s[h, rc * GATE_ROWS + v * SUBLANES:rc * GATE_ROWS + (v + 1) * SUBLANES, sl]
                          for v in range(nvr)]
                    for ii in range(SUBLANES):
                        cut = cutb_r[h, ii, :, sl]
                        a = ab_r[h, ii, :, sl]
                        for v in range(nvr):
                            g[ii][v] = g[ii][v] + jnp.where(s1c[v] >= cut, a * bc[v], 0.0)
                for ii in range(SUBLANES):
                    rows = slice(ii * nk + rc * GATE_ROWS, ii * nk + (rc + 1) * GATE_ROWS)
                    gi = jnp.concatenate(g[ii], axis=0)
                    w_w[rows, sl] = (gi * _gelu(act_r[rows, sl])).astype(BF16)
        fill_rows(cutb_w, ab_w, jnp.minimum(e + 1, nblk - 1))

    @pl.when(jnp.logical_and(e < nblk, e % 2 == 0))
    def _():
        stage(act1_s, act0_s, w0_s, w1_s, cutb1_s, ab1_s, cutb0_s, ab0_s)

    @pl.when(jnp.logical_and(e < nblk, e % 2 == 1))
    def _():
        stage(act0_s, act1_s, w1_s, w0_s, cutb0_s, ab0_s, cutb1_s, ab1_s)

    @pl.when(e == nblk)
    def _():
        w_last = w0_s if (nblk - 1) % 2 == 0 else w1_s
        acc = acc_s[...] + jnp.dot(vT_ref[...], w_last[...], preferred_element_type=F32)
        o_ref[...] = x_ref[...] + g2_ref[0] * acc.T


def _peer(x, sc, sh, g2, ng, wqT, keys, u_b, vT_b, row_of, T, eb):
    n, d = x.shape
    n_exp = u_b.shape[0]
    nk = keys.shape[1]
    nh = PEER_HEADS
    modspec = pl.BlockSpec((1, 1, d), lambda i, e: (row_of(i), 0, 0))
    nblk = n_exp // eb
    assert eb == nk * SUBLANES and nk % GATE_ROWS == 0 and T % LANES == 0
    once = pl.Buffered(1)
    return pl.pallas_call(
        functools.partial(_peer_kernel, nblk=nblk),
        grid=(n // T, nblk + 1),
        in_specs=[
            pl.BlockSpec((T, d), lambda i, e: (i, 0)),
            modspec, modspec, modspec,
            pl.BlockSpec((1, d), lambda i, e: (0, 0)),
            pl.BlockSpec(wqT.shape, lambda i, e: (0, 0), pipeline_mode=once),
            pl.BlockSpec(keys.shape, lambda i, e: (0, 0, 0), pipeline_mode=once),
            pl.BlockSpec((eb, d), lambda i, e: (0, 0), pipeline_mode=once),
            pl.BlockSpec((eb, d), lambda i, e: (jnp.minimum(e + 1, nblk - 1), 0)),
            pl.BlockSpec((d, eb), lambda i, e: (0, jnp.clip(e - 1, 0, nblk - 1))),
        ],
        out_specs=pl.BlockSpec((T, d), lambda i, e: (i, 0)),
        out_shape=jax.ShapeDtypeStruct((n, d), F32),
        scratch_shapes=[
            pltpu.VMEM((d, T), BF16),
            pltpu.VMEM((2 * nh, nk, T), F32),
            pltpu.VMEM((2 * nh, PEER_TOPK, T), F32),
            pltpu.VMEM((nh, PEER_TOPK, T), F32),
            pltpu.VMEM((nh, SUBLANES, T), F32),
            pltpu.VMEM((nh, nk, T), F32),
            pltpu.VMEM((nh, nk, T), F32),
            pltpu.VMEM((nh, nk, T), F32),
            pltpu.VMEM((eb, T), F32),
            pltpu.VMEM((eb, T), F32),
            pltpu.VMEM((eb, T), BF16),
            pltpu.VMEM((eb, T), BF16),
            pltpu.VMEM((nh, SUBLANES, SUBLANES, T), F32),
            pltpu.VMEM((nh, SUBLANES, SUBLANES, T), F32),
            pltpu.VMEM((nh, SUBLANES, SUBLANES, T), F32),
            pltpu.VMEM((nh, SUBLANES, SUBLANES, T), F32),
            pltpu.VMEM((d, T), F32),
        ],
        compiler_params=_cparams("parallel", "arbitrary"),
        name="peer",
    )(x, sc, sh, g2, ng, wqT, keys, u_b, u_b, vT_b)


def _pick(n, pref):
    t = pref
    while n % t:
        t //= 2
    return t


def kernel(x, c, ctx, c_ctx, w_mod, b_mod, norm1_g, norm2_g, ab_w_in, a_ws, a_bs, a_vnorm_g, b_conv_w, ab_w_out,
           cd_w_in, c_short_w, c_short_b, c_filt_w1, c_filt_b1, c_filt_w2, c_filt_b2, c_filt_w3, c_filt_b3,
           c_filt_wout, c_filt_freq, c_skip, d_q_norm_g, d_k_norm_g, d_lambda_q1, d_lambda_k1, d_lambda_q2,
           d_lambda_k2, d_subln_g, cd_w_out, peer_wq, peer_keys, peer_u, peer_v):
    B, L, D = x.shape
    Lc = ctx.shape[1]
    depth = w_mod.shape[0]
    n_lat, n_ctx = B * L, B * Lc

    tm_l, tm_c = _pick(L, 256), _pick(Lc, 256)
    tq_l, tq_c = _pick(L, 256), _pick(Lc, 256)
    tp_l, tp_c = _pick(L, 512), _pick(n_ctx, 512)
    eb = 1024

    mod_rows = 2 * SUBLANES
    s_rows = jnp.zeros((mod_rows, D), F32).at[:B].set(c).at[B].set(c_ctx)
    mod = _mod_all(s_rows, w_mod, b_mod).reshape(depth, mod_rows, N_MOD, 1, D)

    xl = x.reshape(n_lat, D)
    xc = ctx.reshape(n_ctx, D)

    def lat_row(t):
        return lambda i: (i * t) // L

    ctx_row = lambda i: B

    col = jnp.arange(QK_W)
    inv = ROPE_BASE ** (-jnp.arange(0, ROPE_AXIS_DIM, 2, dtype=F32) / ROPE_AXIS_DIM)
    pos = jnp.arange(L)
    rows_f = (pos // GRID_W).astype(F32)
    cols_f = (pos % GRID_W).astype(F32)
    is_row_part = (col % D_QK) < ROPE_AXIS_DIM
    ang = jnp.where(is_row_part[None, :], rows_f[:, None], cols_f[:, None]) * inv[col % (ROPE_AXIS_DIM // 2)][None, :]
    cos_l = jnp.cos(ang)
    sin_l = jnp.where(((col % ROPE_AXIS_DIM) < ROPE_AXIS_DIM // 2)[None, :], -jnp.sin(ang), jnp.sin(ang))
    cos_c = jnp.ones((tm_c, QK_W), F32)
    sin_c = jnp.zeros((tm_c, QK_W), F32)

    deltas = jnp.abs(jnp.linspace(math.log(HY_TARGET) / HY_DECAY_LONG_PCT, math.log(HY_TARGET) / HY_DECAY_SHORT_PCT,
                                  C_WIDTH, dtype=F32))[None, :]

    for l in range(depth):
        last = l == depth - 1
        odd = l % 2 == 1
        i = l // 2
        m = lambda k: mod[l, :, k]
        sh1, sc1, g1, sh2, sc2, g2 = (m(k) for k in range(N_MOD))
        n1 = norm1_g[l][None, :]
        need_ctx = (not last) or odd

        if not odd:
            w_in = ab_w_in[i].astype(BF16)
            w_out = ab_w_out[i].astype(BF16)
            ws = a_ws[i].astype(BF16)
            bias_full = jnp.repeat(a_bs[i].T, A_HEAD_DIM, axis=1)
            vg = a_vnorm_g[i][None, :]
            cw = b_conv_w[i]
            p = _norm_mod_matmul(xl, sc1, sh1, n1, w_in, lat_row(tm_l), tm_l)
            xl = _mix_ab(p, xl, g1, ws, bias_full, vg, cw, w_out, lat_row(tm_l), L, tm_l)
            if not last:
                pc = _norm_mod_matmul(xc, sc1, sh1, n1, w_in, ctx_row, tm_c)
                xc = _mix_ab(pc, xc, g1, ws, bias_full, vg, cw, w_out, ctx_row, Lc, tm_c)
        else:
            lam_init = 0.8 - 0.6 * math.exp(-0.3 * l)
            w_in = cd_w_in[i].astype(BF16)
            w_out = cd_w_out[i].astype(BF16)
            qg = jnp.tile(d_q_norm_g[i], QK_W // D_QK)[None, :]
            kg = jnp.tile(d_k_norm_g[i], QK_W // D_QK)[None, :]
            lam_vecs = tuple(v[i][None, :] for v in (d_lambda_q1, d_lambda_k1, d_lambda_q2, d_lambda_k2))
            sg = d_subln_g[i][None, :]
            fw = LANES
            padc = lambda a: jnp.pad(a, ((0, 0), (0, fw - a.shape[1])))
            padr = lambda a: jnp.pad(a, ((0, fw - a.shape[0]), (0, 0)))
            fp = (padc(padr(c_filt_w1[i])), padc(c_filt_b1[i][None, :]), padc(padr(c_filt_w2[i])),
                  padc(c_filt_b2[i][None, :]), padc(padr(c_filt_w3[i])), padc(c_filt_b3[i][None, :]),
                  padr(c_filt_wout[i]), padc(c_filt_freq[i][None, :]))
            sw, sb, skip = c_short_w[i], c_short_b[i][None, :], c_skip[i][None, :]

            def hyena(pp, batch, seq, tm):
                x0, u, ub = _hy_pre(pp, sw, sb, seq, tm)
                fwd, invm = _dft_mats(seq)
                filt = _hy_filter(seq, fp, deltas)
                spec = _matmul(fwd, filt, _pick(2 * seq, 512))
                y = _hy_conv(ub, fwd, invm, spec, batch, seq, _pick(seq, 256))
                return x0, y, u

            pc = _norm_mod_matmul(xc, sc1, sh1, n1, w_in, ctx_row, tm_c)
            qmc, kc = _qk_prep(pc, cos_c, sin_c, qg, kg, lambda t: 0, tm_c)
            vc = pc[:, HY_IN + 2 * QK_W:].astype(BF16)
            p = _norm_mod_matmul(xl, sc1, sh1, n1, w_in, lat_row(tm_l), tm_l)
            qm, k = _qk_prep(p, cos_l, sin_l, qg, kg, lambda t: t % (L // tm_l), tm_l)
            v = p[:, HY_IN + 2 * QK_W:].astype(BF16)
            k_all = jnp.concatenate([kc.reshape(B, Lc, QK_W), k.reshape(B, L, QK_W)], axis=1)
            v_all = jnp.concatenate([vc.reshape(B, Lc, V_W), v.reshape(B, L, V_W)], axis=1)
            yd = _diff_attn(qm, k_all, v_all, lam_vecs, sg, lam_init, B, L, tq_l)
            x0, y, u = hyena(p, B, L, tm_l)
            xl = _proj_cd(x0, y, u, skip, yd, w_out, xl, g1, lat_row(tm_l), tm_l)
            if not last:
                ydc = _diff_attn(qmc, kc.reshape(B, Lc, QK_W), vc.reshape(B, Lc, V_W), lam_vecs, sg, lam_init,
                                 B, Lc, tq_c)
                x0c, yc, uc = hyena(pc, B, Lc, tm_c)
                xc = _proj_cd(x0c, yc, uc, skip, ydc, w_out, xc, g1, ctx_row, tm_c)

        n2 = norm2_g[l][None, :]
        wqT = peer_wq[l].T.astype(BF16)
        keys = peer_keys[l].reshape(2 * PEER_HEADS, peer_keys.shape[3], peer_keys.shape[4]).astype(BF16)
        u_b = peer_u[l].astype(BF16)
        vT_b = peer_v[l].T.astype(BF16)
        xl = _peer(xl, sc2, sh2, g2, n2, wqT, keys, u_b, vT_b, lat_row(tp_l), tp_l, eb)
        if not last:
            xc = _peer(xc, sc2, sh2, g2, n2, wqT, keys, u_b, vT_b, ctx_row, tp_c, eb)

    return xl.reshape(B, L, D)
```

```python
import functools
import math

import jax
import jax.numpy as jnp
from jax import lax
from jax.experimental import pallas as pl
from jax.experimental.pallas import tpu as pltpu

F32 = jnp.float32
BF16 = jnp.bfloat16
HI = lax.Precision.HIGHEST

GRID_W = 64
N_MOD = 6
RMS_EPS = 1e-6
CHUNK = 128
A_HEADS = 8
A_HEAD_DIM = 64
A_WIDTH = A_HEADS * A_HEAD_DIM
B_WIDTH = 512
C_WIDTH = 512
HY_IN = 3 * C_WIDTH
HY_EMB = 33
HY_DECAY_LONG_PCT = 1.5
HY_DECAY_SHORT_PCT = 0.3
HY_TARGET = 1e-2
D_HEADS = 4
D_QK = 64
D_V = 2 * D_QK
QK_W = D_HEADS * 2 * D_QK
V_W = D_HEADS * D_V
ATTN_SCALE = D_QK ** -0.5
ROPE_AXIS_DIM = D_QK // 2
ROPE_BASE = 10000.0
PEER_HEADS = 8
PEER_TOPK = 16

LANES = 128
SUBLANES = 8
VMEM_LIMIT = 56 * 1024 * 1024
GATE_ROWS = 32


def _cparams(*sem):
    return pltpu.CompilerParams(dimension_semantics=sem, vmem_limit_bytes=VMEM_LIMIT)


def _gelu(x):
    return 0.5 * x * (1.0 + lax.erf(x * (2.0 ** -0.5)))


def _oddeven_merge_sort_pairs(n):
    pairs, p = [], 1
    while p < n:
        k = p
        while k >= 1:
            for j in range(k % p, n - k, 2 * k):
                for i in range(min(k, n - j - k)):
                    if (i + j) // (2 * p) == (i + j + k) // (2 * p):
                        pairs.append((i + j, i + j + k))
            k //= 2
        p *= 2
    return pairs


def _bitonic_merge_pairs(n):
    pairs, d = [], n // 2
    while d >= 1:
        pairs += [(i, i + d) for i in range(n) if (i & d) == 0]
        d //= 2
    return pairs


_SORT_PAIRS = _oddeven_merge_sort_pairs(PEER_TOPK)
_MERGE_PAIRS = _bitonic_merge_pairs(PEER_TOPK)


def _group_mean_sq(v, group):
    w = v.shape[-1]
    r = lax.broadcasted_iota(jnp.int32, (w, w), 0) // group
    c = lax.broadcasted_iota(jnp.int32, (w, w), 1) // group
    bd = jnp.where(r == c, 1.0, 0.0).astype(BF16)
    vv = v * v
    hi = vv.astype(BF16)
    lo = (vv - hi.astype(F32)).astype(BF16)
    s = jnp.dot(hi, bd, preferred_element_type=F32) + jnp.dot(lo, bd, preferred_element_type=F32)
    return s * (1.0 / group)


def _mod_kernel(s_ref, w_ref, b_ref, o_ref):
    s = s_ref[...]
    s = s / (1.0 + jnp.exp(-s))
    o_ref[0] = jnp.dot(s, w_ref[0], precision=HI, preferred_element_type=F32) + b_ref[0]


def _mod_all(s_rows, w_mod, b_mod):
    depth, d, n6 = w_mod.shape
    rows = s_rows.shape[0]
    tn = n6 // 4
    return pl.pallas_call(
        _mod_kernel,
        grid=(depth, n6 // tn),
        in_specs=[
            pl.BlockSpec((rows, d), lambda l, j: (0, 0)),
            pl.BlockSpec((1, d, tn), lambda l, j: (l, 0, j)),
            pl.BlockSpec((1, 1, tn), lambda l, j: (l, 0, j)),
        ],
        out_specs=pl.BlockSpec((1, rows, tn), lambda l, j: (l, 0, j)),
        out_shape=jax.ShapeDtypeStruct((depth, rows, n6), F32),
        compiler_params=_cparams("arbitrary", "arbitrary"),
        name="mod",
    )(s_rows, w_mod, b_mod.reshape(depth, 1, n6))


def _norm_mod(x, g, sc, sh):
    ms = jnp.mean(x * x, axis=-1, keepdims=True)
    return (x * lax.rsqrt(ms + RMS_EPS) * g) * (1.0 + sc) + sh


def _nmm_kernel(x_ref, sc_ref, sh_ref, g_ref, w_ref, o_ref):
    h = _norm_mod(x_ref[...], g_ref[...], sc_ref[0], sh_ref[0])
    o_ref[...] = jnp.dot(h.astype(BF16), w_ref[...], preferred_element_type=F32)


def _norm_mod_matmul(x, sc, sh, g, w, row_of, tm):
    n, d = x.shape
    nout = w.shape[1]
    return pl.pallas_call(
        _nmm_kernel,
        grid=(n // tm,),
        in_specs=[
            pl.BlockSpec((tm, d), lambda i: (i, 0)),
            pl.BlockSpec((1, 1, d), lambda i: (row_of(i), 0, 0)),
            pl.BlockSpec((1, 1, d), lambda i: (row_of(i), 0, 0)),
            pl.BlockSpec((1, d), lambda i: (0, 0)),
            pl.BlockSpec((d, nout), lambda i: (0, 0)),
        ],
        out_specs=pl.BlockSpec((tm, nout), lambda i: (i, 0)),
        out_shape=jax.ShapeDtypeStruct((n, nout), F32),
        compiler_params=_cparams("parallel"),
        name="norm_mod_inproj",
    )(x, sc, sh, g, w)


def _shift_rows(z, prev_row, next_row):
    tm = z.shape[0]
    row = lax.broadcasted_iota(jnp.int32, z.shape, 0)
    up = jnp.where(row == 0, prev_row, pltpu.roll(z, 1, 0))
    dn = jnp.where(row == tm - 1, next_row, pltpu.roll(z, tm - 1, 0))
    return up, dn


def _mix_ab_kernel(p_ref, hbp_ref, gcp_ref, hbn_ref, gcn_ref, x_ref, g1_ref, ws_ref, bias_ref, vg_ref, cw_ref,
                   wout_ref, o_ref, *, tiles_per_seq):
    i = pl.program_id(0)
    tm = p_ref.shape[0]
    aw, bw = A_WIDTH, B_WIDTH
    u = _gelu(p_ref[:, 0:aw])
    v = _gelu(p_ref[:, aw:2 * aw])
    vn = (v * lax.rsqrt(_group_mean_sq(v, A_HEAD_DIM) + RMS_EPS) * vg_ref[...]).astype(BF16)
    sv_chunks = []
    for cidx in range(tm // CHUNK):
        vc = vn[cidx * CHUNK:(cidx + 1) * CHUNK]
        heads = [jnp.dot(ws_ref[h], vc[:, h * A_HEAD_DIM:(h + 1) * A_HEAD_DIM], preferred_element_type=F32)
                 for h in range(A_HEADS)]
        sv_chunks.append(jnp.concatenate(heads, axis=-1) + bias_ref[...])
    sv = jnp.concatenate(sv_chunks, axis=0) if len(sv_chunks) > 1 else sv_chunks[0]
    ya = u * sv

    o0 = 2 * aw
    hb = p_ref[:, o0:o0 + bw]
    gb = p_ref[:, o0 + bw:o0 + 2 * bw]
    gc = p_ref[:, o0 + 2 * bw:o0 + 3 * bw]
    z = gc * hb
    pos = i % tiles_per_seq
    zp = jnp.where(pos == 0, 0.0, gcp_ref[SUBLANES - 1:SUBLANES, :] * hbp_ref[SUBLANES - 1:SUBLANES, :])
    zn = jnp.where(pos == tiles_per_seq - 1, 0.0, gcn_ref[0:1, :] * hbn_ref[0:1, :])
    z_up, z_dn = _shift_rows(z, zp, zn)
    yb = gb * (cw_ref[0:1, :] * z_up + cw_ref[1:2, :] * z + cw_ref[2:3, :] * z_dn)

    y = jnp.concatenate([ya, yb], axis=-1).astype(BF16)
    o_ref[...] = x_ref[...] + g1_ref[0] * jnp.dot(y, wout_ref[...], preferred_element_type=F32)


def _mix_ab(p, x, g1, ws, bias_full, vg, cw, w_out, row_of, seq_len, tm):
    n, d = x.shape
    nin = p.shape[1]
    nb8 = n // SUBLANES
    r8 = tm // SUBLANES
    hb_blk = (2 * A_WIDTH) // B_WIDTH
    gc_blk = hb_blk + 2
    prev = lambda i: jnp.maximum(i * r8 - 1, 0)
    nxt = lambda i: jnp.minimum((i + 1) * r8, nb8 - 1)
    full = lambda a: pl.BlockSpec(a.shape, lambda i: (0,) * a.ndim)
    return pl.pallas_call(
        functools.partial(_mix_ab_kernel, tiles_per_seq=seq_len // tm),
        grid=(n // tm,),
        in_specs=[
            pl.BlockSpec((tm, nin), lambda i: (i, 0)),
            pl.BlockSpec((SUBLANES, B_WIDTH), lambda i: (prev(i), hb_blk)),
            pl.BlockSpec((SUBLANES, B_WIDTH), lambda i: (prev(i), gc_blk)),
            pl.BlockSpec((SUBLANES, B_WIDTH), lambda i: (nxt(i), hb_blk)),
            pl.BlockSpec((SUBLANES, B_WIDTH), lambda i: (nxt(i), gc_blk)),
            pl.BlockSpec((tm, d), lambda i: (i, 0)),
            pl.BlockSpec((1, 1, d), lambda i: (row_of(i), 0, 0)),
            full(ws), full(bias_full), full(vg), full(cw), full(w_out),
        ],
        out_specs=pl.BlockSpec((tm, d), lambda i: (i, 0)),
        out_shape=jax.ShapeDtypeStruct((n, d), F32),
        compiler_params=_cparams("parallel"),
        name="mix_ab",
    )(p, p, p, p, p, x, g1, ws, bias_full, vg, cw, w_out)


def _qk_prep_kernel(q_ref, k_ref, cos_ref, sin_ref, qg_ref, kg_ref, qm_ref, ko_ref):
    lane = lax.broadcasted_iota(jnp.int32, q_ref.shape, 1)
    first = (lane % ROPE_AXIS_DIM) < (ROPE_AXIS_DIM // 2)
    w = q_ref.shape[1]
    half = ROPE_AXIS_DIM // 2

    def prep(t, g):
        t = t * lax.rsqrt(_group_mean_sq(t, D_QK) + RMS_EPS) * g
        partner = jnp.where(first, pltpu.roll(t, w - half, 1), pltpu.roll(t, half, 1))
        return t * cos_ref[...] + partner * sin_ref[...]

    q = prep(q_ref[...], qg_ref[...]) * (ATTN_SCALE * math.log2(math.e))
    in_h0 = (lane % (2 * D_QK)) < D_QK
    qm_ref[0] = jnp.where(in_h0, q, 0.0).astype(BF16)
    qm_ref[1] = jnp.where(in_h0, 0.0, q).astype(BF16)
    ko_ref[...] = prep(k_ref[...], kg_ref[...]).astype(BF16)


def _qk_prep(p, cos_t, sin_t, qg, kg, tab_of, tm):
    n = p.shape[0]
    q_blk = HY_IN // QK_W
    return pl.pallas_call(
        _qk_prep_kernel,
        grid=(n // tm,),
        in_specs=[
            pl.BlockSpec((tm, QK_W), lambda i: (i, q_blk)),
            pl.BlockSpec((tm, QK_W), lambda i: (i, q_blk + 1)),
            pl.BlockSpec((tm, QK_W), lambda i: (tab_of(i), 0)),
            pl.BlockSpec((tm, QK_W), lambda i: (tab_of(i), 0)),
            pl.BlockSpec((1, QK_W), lambda i: (0, 0)),
            pl.BlockSpec((1, QK_W), lambda i: (0, 0)),
        ],
        out_specs=[
            pl.BlockSpec((2, tm, QK_W), lambda i: (0, i, 0)),
            pl.BlockSpec((tm, QK_W), lambda i: (i, 0)),
        ],
        out_shape=[jax.ShapeDtypeStruct((2, n, QK_W), BF16), jax.ShapeDtypeStruct((n, QK_W), BF16)],
        compiler_params=_cparams("parallel"),
        name="qk_prep",
    )(p, p, cos_t, sin_t, qg, kg)


def _diff_attn_kernel(q0_ref, q1_ref, k_ref, v_ref, lq1_ref, lk1_ref, lq2_ref, lk2_ref, sg_ref, o_ref, *, lam_init):
    lam = (jnp.exp(jnp.sum(lq1_ref[...] * lk1_ref[...], axis=-1, keepdims=True))
           - jnp.exp(jnp.sum(lq2_ref[...] * lk2_ref[...], axis=-1, keepdims=True)) + lam_init)
    tq = q0_ref.shape[1]
    q = jnp.concatenate([q0_ref[0], q1_ref[0]], axis=0)
    s = lax.dot_general(q, k_ref[0], (((1,), (1,)), ((), ())), preferred_element_type=F32)
    m = jnp.max(s, axis=-1, keepdims=True)
    p = jnp.exp2(s - m)
    l = jnp.sum(p, axis=-1, keepdims=True)
    pv = jnp.dot(p.astype(BF16), v_ref[0], preferred_element_type=F32) / l
    o = pv[:tq] - lam * pv[tq:]
    ms = jnp.mean(o * o, axis=-1, keepdims=True)
    o_ref[...] = (o * lax.rsqrt(ms + RMS_EPS) * sg_ref[...]) * (1.0 - lam_init)


def _diff_attn(qm, k_all, v_all, lam_vecs, subln_g, lam_init, batch, lq, tq):
    lk = k_all.shape[1]
    nq = lq // tq
    vec = lambda a: pl.BlockSpec(a.shape, lambda b, h, i: (0, 0))
    return pl.pallas_call(
        functools.partial(_diff_attn_kernel, lam_init=lam_init),
        grid=(batch, D_HEADS, nq),
        in_specs=[
            pl.BlockSpec((1, tq, D_V), lambda b, h, i: (0, b * nq + i, h)),
            pl.BlockSpec((1, tq, D_V), lambda b, h, i: (1, b * nq + i, h)),
            pl.BlockSpec((1, lk, D_V), lambda b, h, i: (b, 0, h)),
            pl.BlockSpec((1, lk, D_V), lambda b, h, i: (b, 0, h)),
            vec(lam_vecs[0]), vec(lam_vecs[1]), vec(lam_vecs[2]), vec(lam_vecs[3]), vec(subln_g),
        ],
        out_specs=pl.BlockSpec((tq, D_V), lambda b, h, i: (b * nq + i, h)),
        out_shape=jax.ShapeDtypeStruct((batch * lq, V_W), F32),
        compiler_params=_cparams("parallel", "parallel", "arbitrary"),
        name="diff_attn",
    )(qm, qm, k_all, v_all, *lam_vecs, subln_g)


def _hy_pre_kernel(p_ref, pp_ref, pn_ref, sw_ref, sb_ref, x0_ref, u_ref, ub_ref, *, tiles_per_seq):
    i = pl.program_id(0)
    pos = i % tiles_per_seq
    z = p_ref[...]
    zp = jnp.where(pos == 0, 0.0, pp_ref[SUBLANES - 1:SUBLANES, :])
    zn = jnp.where(pos == tiles_per_seq - 1, 0.0, pn_ref[0:1, :])
    z_up, z_dn = _shift_rows(z, zp, zn)
    c = sw_ref[0:1, :] * z_up + sw_ref[1:2, :] * z + sw_ref[2:3, :] * z_dn + sb_ref[...]
    cw = C_WIDTH
    u = c[:, cw:2 * cw] * c[:, 2 * cw:3 * cw]
    x0_ref[...] = c[:, 0:cw]
    u_ref[...] = u
    ub_ref[...] = u.astype(BF16)


def _hy_pre(p, sw, sb, seq_len, tm):
    n = p.shape[0]
    nb8 = n // SUBLANES
    r8 = tm // SUBLANES
    prev = lambda i: jnp.maximum(i * r8 - 1, 0)
    nxt = lambda i: jnp.minimum((i + 1) * r8, nb8 - 1)
    o_spec = pl.BlockSpec((tm, C_WIDTH), lambda i: (i, 0))
    return pl.pallas_call(
        functools.partial(_hy_pre_kernel, tiles_per_seq=seq_len // tm),
        grid=(n // tm,),
        in_specs=[
            pl.BlockSpec((tm, HY_IN), lambda i: (i, 0)),
            pl.BlockSpec((SUBLANES, HY_IN), lambda i: (prev(i), 0)),
            pl.BlockSpec((SUBLANES, HY_IN), lambda i: (nxt(i), 0)),
            pl.BlockSpec((3, HY_IN), lambda i: (0, 0)),
            pl.BlockSpec((1, HY_IN), lambda i: (0, 0)),
        ],
        out_specs=[o_spec, o_spec, o_spec],
        out_shape=[jax.ShapeDtypeStruct((n, C_WIDTH), F32), jax.ShapeDtypeStruct((n, C_WIDTH), F32),
                   jax.ShapeDtypeStruct((n, C_WIDTH), BF16)],
        compiler_params=_cparams("parallel"),
        name="hyena_pre",
    )(p, p, p, sw, sb)


def _hy_filter_kernel(w1_ref, b1_ref, w2_ref, b2_ref, w3_ref, b3_ref, wo_ref, fr_ref, dl_ref, o_ref):
    L = o_ref.shape[0]
    fw = w1_ref.shape[0]
    bands = (HY_EMB - 1) // 2
    row = lax.broadcasted_iota(jnp.int32, (L, fw), 0).astype(F32)
    col = lax.broadcasted_iota(jnp.int32, (L, fw), 1)
    t = row / (L - 1.0)
    ang = (2.0 * math.pi) * row / L
    step = (bands - 1 - 1e-4) / (bands - 1)
    band = jnp.where(col <= bands, col - 1, col - 1 - bands).astype(F32)
    arg = (1e-4 + band * step) * ang
    feats = jnp.where(col == 0, t, jnp.where(col <= bands, jnp.cos(arg), jnp.where(col < HY_EMB, -jnp.sin(arg), 0.0)))
    fr = fr_ref[...]
    mm = lambda a, w: jnp.dot(a, w, precision=HI, preferred_element_type=F32)
    a = jnp.sin(fr * (mm(feats, w1_ref[...]) + b1_ref[...]))
    a = jnp.sin(fr * (mm(a, w2_ref[...]) + b2_ref[...]))
    a = jnp.sin(fr * (mm(a, w3_ref[...]) + b3_ref[...]))
    hf = mm(a, wo_ref[...])
    rowc = lax.broadcasted_iota(jnp.int32, (L, C_WIDTH), 0)
    tc = rowc.astype(F32) / (L - 1.0)
    decay = jnp.exp(-tc * dl_ref[...])
    h_fwd = hf[:, :C_WIDTH] * decay
    h_bwd = jnp.where(rowc == 0, 0.0, hf[:, C_WIDTH:] * decay)
    norm = jnp.sum(jnp.abs(h_fwd) + jnp.abs(h_bwd), axis=0, keepdims=True)
    o_ref[:, :C_WIDTH] = (h_fwd / norm).astype(BF16)
    o_ref[:, C_WIDTH:] = (h_bwd / norm).astype(BF16)


def _hy_filter(L, fp, deltas):
    full = lambda a: pl.BlockSpec(a.shape, lambda: (0,) * a.ndim)
    args = (*fp, deltas)
    return pl.pallas_call(
        _hy_filter_kernel,
        in_specs=[full(a) for a in args],
        out_specs=pl.BlockSpec((L, 2 * C_WIDTH), lambda: (0, 0)),
        out_shape=jax.ShapeDtypeStruct((L, 2 * C_WIDTH), BF16),
        compiler_params=pltpu.CompilerParams(vmem_limit_bytes=VMEM_LIMIT),
        name="hyena_filter",
    )(*args)


def _mm_kernel(a_ref, b_ref, o_ref):
    o_ref[...] = jnp.dot(a_ref[...], b_ref[...], preferred_element_type=F32)


def _matmul(a, b, tm):
    m, k = a.shape
    n = b.shape[1]
    return pl.pallas_call(
        _mm_kernel,
        grid=(m // tm,),
        in_specs=[pl.BlockSpec((tm, k), lambda i: (i, 0)), pl.BlockSpec((k, n), lambda i: (0, 0))],
        out_specs=pl.BlockSpec((tm, n), lambda i: (i, 0)),
        out_shape=jax.ShapeDtypeStruct((m, n), F32),
        compiler_params=_cparams("parallel"),
        name="dft_filter",
    )(a, b)


def _hy_conv_kernel(u_ref, wt_ref, wb_ref, w2t_ref, w2b_ref, at_ref, ab_ref, y_ref):
    j = pl.program_id(1)
    cw = C_WIDTH
    u = u_ref[...]
    uc = jnp.dot(wt_ref[...], u, preferred_element_type=F32)
    us = jnp.dot(wb_ref[...], u, preferred_element_type=F32)
    kre = at_ref[:, :cw] + at_ref[:, cw:]
    kim = ab_ref[:, cw:] - ab_ref[:, :cw]
    knyq = ab_ref[:, :cw] + ab_ref[:, cw:]
    row = lax.broadcasted_iota(jnp.int32, kre.shape, 0)
    special = jnp.logical_and(row == 0, j == 0)
    p2 = jnp.where(special, 0.0, kim)
    p4 = jnp.where(special, knyq, -kre)
    yt = (uc * kre + us * p2).astype(BF16)
    yb = (uc * p2 + us * p4).astype(BF16)
    L = y_ref.shape[0]
    tr = min(L, 512)

    def inverse_rows(rows, first):
        contrib = (jnp.dot(w2t_ref[rows, :], yt, preferred_element_type=F32)
                   + jnp.dot(w2b_ref[rows, :], yb, preferred_element_type=F32))
        y_ref[rows, :] = contrib if first else y_ref[rows, :] + contrib

    @pl.when(j == 0)
    def _():
        for c in range(L // tr):
            inverse_rows(slice(c * tr, (c + 1) * tr), True)

    @pl.when(j > 0)
    def _():
        for c in range(L // tr):
            inverse_rows(slice(c * tr, (c + 1) * tr), False)


def _hy_conv(ub, w, w2, a_spec, batch, L, rb):
    ns = L // rb
    return pl.pallas_call(
        _hy_conv_kernel,
        grid=(batch, ns),
        in_specs=[
            pl.BlockSpec((L, C_WIDTH), lambda b, j: (b, 0)),
            pl.BlockSpec((rb, L), lambda b, j: (j, 0)),
            pl.BlockSpec((rb, L), lambda b, j: (j + ns, 0)),
            pl.BlockSpec((L, rb), lambda b, j: (0, j)),
            pl.BlockSpec((L, rb), lambda b, j: (0, j + ns)),
            pl.BlockSpec((rb, 2 * C_WIDTH), lambda b, j: (j, 0)),
            pl.BlockSpec((rb, 2 * C_WIDTH), lambda b, j: (j + ns, 0)),
        ],
        out_specs=pl.BlockSpec((L, C_WIDTH), lambda b, j: (b, 0)),
        out_shape=jax.ShapeDtypeStruct((batch * L, C_WIDTH), F32),
        compiler_params=_cparams("parallel", "arbitrary"),
        name="hyena_conv",
    )(ub, w, w, w2, w2, a_spec, a_spec)


def _proj_cd_kernel(x0_ref, y_ref, u_ref, skip_ref, yd_ref, w_ref, x_ref, g1_ref, o_ref):
    yh = (x0_ref[...] * (y_ref[...] + u_ref[...] * skip_ref[...])).astype(BF16)
    acc = jnp.dot(yh, w_ref[:C_WIDTH, :], preferred_element_type=F32)
    acc += jnp.dot(yd_ref[...].astype(BF16), w_ref[C_WIDTH:, :], preferred_element_type=F32)
    o_ref[...] = x_ref[...] + g1_ref[0] * acc


def _proj_cd(x0, y, u, skip, yd, w_out, x, g1, row_of, tm):
    n, d = x.shape
    half = pl.BlockSpec((tm, C_WIDTH), lambda i: (i, 0))
    return pl.pallas_call(
        _proj_cd_kernel,
        grid=(n // tm,),
        in_specs=[
            half, half, half,
            pl.BlockSpec((1, C_WIDTH), lambda i: (0, 0)),
            half,
            pl.BlockSpec(w_out.shape, lambda i: (0, 0)),
            pl.BlockSpec((tm, d), lambda i: (i, 0)),
            pl.BlockSpec((1, 1, d), lambda i: (row_of(i), 0, 0)),
        ],
        out_specs=pl.BlockSpec((tm, d), lambda i: (i, 0)),
        out_shape=jax.ShapeDtypeStruct((n, d), F32),
        compiler_params=_cparams("parallel"),
        name="proj_cd",
    )(x0, y, u, skip, yd, w_out, x, g1)


def _dft_mats(L):
    m = 2 * L
    r = jnp.arange(L, dtype=jnp.int32)
    blk = math.gcd(L, 64)
    phase = lambda cols: ((r[:, None] * cols[None, :]) % m).astype(F32) * (2.0 * math.pi / m)
    a_hi = phase(jnp.arange(L // blk, dtype=jnp.int32) * blk)[:, :, None]
    a_lo = phase(jnp.arange(blk, dtype=jnp.int32))[:, None, :]
    c = (jnp.cos(a_hi) * jnp.cos(a_lo) - jnp.sin(a_hi) * jnp.sin(a_lo)).reshape(L, L)
    s = (jnp.sin(a_hi) * jnp.cos(a_lo) + jnp.cos(a_hi) * jnp.sin(a_lo)).reshape(L, L)
    alt = jnp.where(r % 2 == 0, 1.0, -1.0).astype(F32)
    is0 = (r == 0)[:, None]
    fwd = jnp.concatenate([c, jnp.where(is0, alt[None, :], s)], axis=0)
    inv_top = jnp.where(is0, 1.0 / m, c * (2.0 / m))
    inv_bot = jnp.where(is0, alt[None, :] / m, s * (-2.0 / m))
    inv = jnp.concatenate([inv_top, inv_bot], axis=0).T
    return fwd.astype(BF16), inv.astype(BF16)


def _peer_kernel(x_ref, sc_ref, sh_ref, g2_ref, ng_ref, wqT_ref, keys_ref, u0_ref, u_ref, vT_ref, o_ref,
                 h2T_s, sT_s, tops_s, th_s, iz_s, aT_s, bT_s, thT_s, act0_s, act1_s, w0_s, w1_s,
                 cutb0_s, cutb1_s, ab0_s, ab1_s, acc_s, *, nblk):
    e = pl.program_id(1)
    nk = keys_ref.shape[1]
    T = x_ref.shape[0]
    eb = u_ref.shape[0]
    nh = PEER_HEADS
    ntc = T // LANES
    neg = -jnp.inf

    def fill_rows(cutb, ab, blk):
        base = blk * SUBLANES if isinstance(blk, int) else pl.multiple_of(blk * SUBLANES, SUBLANES)
        for h in range(nh):
            cut8 = thT_s[h, pl.ds(base, SUBLANES), :]
            a8 = aT_s[h, pl.ds(base, SUBLANES), :]
            for ii in range(SUBLANES):
                cutb[h, ii] = jnp.broadcast_to(cut8[ii:ii + 1, :], (SUBLANES, T))
                ab[h, ii] = jnp.broadcast_to(a8[ii:ii + 1, :], (SUBLANES, T))

    @pl.when(e == 0)
    def _route():
        h2 = _norm_mod(x_ref[...], ng_ref[...], sc_ref[0], sh_ref[0])
        h2T = h2.T.astype(BF16)
        h2T_s[...] = h2T
        qT = jnp.dot(wqT_ref[...], h2T, preferred_element_type=F32).astype(BF16)
        pk = keys_ref.shape[2]
        for hp in range(2 * nh):
            sT_s[hp] = jnp.dot(keys_ref[hp], qT[hp * pk:(hp + 1) * pk, :], preferred_element_type=F32)

        def exchange(v, pairs):
            for i, j in pairs:
                v[i], v[j] = jnp.maximum(v[i], v[j]), jnp.minimum(v[i], v[j])

        def top_body(hp, carry):
            for tc in range(ntc):
                sl = slice(tc * LANES, (tc + 1) * LANES)
                v = [sT_s[hp, i * SUBLANES:(i + 1) * SUBLANES, sl] for i in range(PEER_TOPK)]
                exchange(v, _SORT_PAIRS)
                shift = SUBLANES // 2
                while shift >= 1:
                    w = [pltpu.roll(a, shift, 0) for a in v]
                    v = [jnp.maximum(v[i], w[PEER_TOPK - 1 - i]) for i in range(PEER_TOPK)]
                    exchange(v, _MERGE_PAIRS)
                    shift //= 2
                for i in range(PEER_TOPK):
                    tops_s[hp, i:i + 1, sl] = v[i][0:1]
            return carry

        lax.fori_loop(0, 2 * nh, top_body, 0)

        act0_s[...] = jnp.dot(u0_ref[...], h2T, preferred_element_type=F32)

        pairs = [(i, j) for i in range(PEER_TOPK) for j in range(PEER_TOPK // (i + 1))]
        for tc in range(ntc):
            sl = slice(tc * LANES, (tc + 1) * LANES)
            A = [jnp.concatenate([tops_s[2 * h, i:i + 1, sl] for h in range(nh)], axis=0) for i in range(PEER_TOPK)]
            B = [jnp.concatenate([tops_s[2 * h + 1, j:j + 1, sl] for h in range(nh)], axis=0)
                 for j in range(PEER_TOPK)]
            cand = [A[i] + B[j] for (i, j) in pairs]
            work = list(cand)
            tau = None
            for r in range(PEER_TOPK):
                tau = functools.reduce(jnp.maximum, work)
                if r < PEER_TOPK - 1:
                    work = [jnp.where(c >= tau, neg, c) for c in work]
            m0 = cand[0]
            z = functools.reduce(lambda a, b: a + b, [jnp.where(c >= tau, jnp.exp(c - m0), 0.0) for c in cand])
            iz = 1.0 / z
            for h in range(nh):
                iz_s[h, 0:1, sl] = iz[h:h + 1]
            for i in range(PEER_TOPK):
                row = [jnp.where(cand[n] >= tau, B[j], jnp.inf) for n, (ii, j) in enumerate(pairs) if ii == i]
                cut = functools.reduce(jnp.minimum, row)
                for h in range(nh):
                    th_s[h, i:i + 1, sl] = cut[h:h + 1]

        def head_body(h, carry):
            s0 = sT_s[2 * h]
            s1 = sT_s[2 * h + 1]
            a0 = tops_s[2 * h, 0:1, :]
            b0 = tops_s[2 * h + 1, 0:1, :]
            aT_s[h] = jnp.exp(s0 - a0) * (0.5 * iz_s[h, 0:1, :])
            bT_s[h] = jnp.exp(s1 - b0)
            th = jnp.full(s0.shape, jnp.inf, F32)
            for i in range(PEER_TOPK):
                th = jnp.where(s0 == tops_s[2 * h, i:i + 1, :], th_s[h, i:i + 1, :], th)
            thT_s[h] = th
            return carry

        lax.fori_loop(0, nh, head_body, 0)

        w1_s[...] = jnp.zeros(w1_s.shape, BF16)
        acc_s[...] = jnp.zeros(acc_s.shape, F32)
        fill_rows(cutb0_s, ab0_s, 0)

    def stage(act_w, act_r, w_w, w_r, cutb_w, ab_w, cutb_r, ab_r):
        nrc = nk // GATE_ROWS
        nvr = GATE_ROWS // SUBLANES
        mrows = eb // ntc
        drows = acc_s.shape[0] // ntc
        for tc in range(ntc):
            sl = slice(tc * LANES, (tc + 1) * LANES)
            ru = slice(tc * mrows, (tc + 1) * mrows)
            act_w[ru, :] = jnp.dot(u_ref[ru, :], h2T_s[...], preferred_element_type=F32)
            rd = slice(tc * drows, (tc + 1) * drows)
            acc_s[rd, :] += jnp.dot(vT_ref[rd, :], w_r[...], preferred_element_type=F32)
            for rc in range(nrc):
                g = [[jnp.zeros((SUBLANES, LANES), F32) for _ in range(nvr)] for _ in range(SUBLANES)]
                for h in range(nh):
                    s1c = [sT_s[2 * h + 1, rc * GATE_ROWS + v * SUBLANES:rc * GATE_ROWS + (v + 1) * SUBLANES, sl]
                           for v in range(nvr)]
                    bc = [bT_s[h, rc * GATE_ROWS + v * SUBLANES:rc * GATE_ROWS + (v + 1) * SUBLANES, sl]
                          for v in range(nvr)]
                    for ii in range(SUBLANES):
                        cut = cutb_r[h, ii, :, sl]
                        a = ab_r[h, ii, :, sl]
                        for v in range(nvr):
                            g[ii][v] = g[ii][v] + jnp.where(s1c[v] >= cut, a * bc[v], 0.0)
                for ii in range(SUBLANES):
                    rows = slice(ii * nk + rc * GATE_ROWS, ii * nk + (rc + 1) * GATE_ROWS)
                    gi = jnp.concatenate(g[ii], axis=0)
                    act = act_r[rows, sl]
                    w_w[rows, sl] = (gi * (act * (1.0 + lax.erf(act * (2.0 ** -0.5))))).astype(BF16)
        fill_rows(cutb_w, ab_w, jnp.minimum(e + 1, nblk - 1))

    @pl.when(jnp.logical_and(e < nblk, e % 2 == 0))
    def _():
        stage(act1_s, act0_s, w0_s, w1_s, cutb1_s, ab1_s, cutb0_s, ab0_s)

    @pl.when(jnp.logical_and(e < nblk, e % 2 == 1))
    def _():
        stage(act0_s, act1_s, w1_s, w0_s, cutb0_s, ab0_s, cutb1_s, ab1_s)

    @pl.when(e == nblk)
    def _():
        w_last = w0_s if (nblk - 1) % 2 == 0 else w1_s
        acc = acc_s[...] + jnp.dot(vT_ref[...], w_last[...], preferred_element_type=F32)
        o_ref[...] = x_ref[...] + g2_ref[0] * acc.T


def _peer(x, sc, sh, g2, ng, wqT, keys, u_b, vT_b, row_of, T, eb):
    n, d = x.shape
    n_exp = u_b.shape[0]
    nk = keys.shape[1]
    nh = PEER_HEADS
    modspec = pl.BlockSpec((1, 1, d), lambda i, e: (row_of(i), 0, 0))
    nblk = n_exp // eb
    assert eb == nk * SUBLANES and nk % GATE_ROWS == 0 and T % LANES == 0 and nk == SUBLANES * PEER_TOPK
    once = pl.Buffered(1)
    return pl.pallas_call(
        functools.partial(_peer_kernel, nblk=nblk),
        grid=(n // T, nblk + 1),
        in_specs=[
            pl.BlockSpec((T, d), lambda i, e: (i, 0)),
            modspec, modspec, modspec,
            pl.BlockSpec((1, d), lambda i, e: (0, 0)),
            pl.BlockSpec(wqT.shape, lambda i, e: (0, 0), pipeline_mode=once),
            pl.BlockSpec(keys.shape, lambda i, e: (0, 0, 0), pipeline_mode=once),
            pl.BlockSpec((eb, d), lambda i, e: (0, 0), pipeline_mode=once),
            pl.BlockSpec((eb, d), lambda i, e: (jnp.minimum(e + 1, nblk - 1), 0)),
            pl.BlockSpec((d, eb), lambda i, e: (0, jnp.clip(e - 1, 0, nblk - 1))),
        ],
        out_specs=pl.BlockSpec((T, d), lambda i, e: (i, 0)),
        out_shape=jax.ShapeDtypeStruct((n, d), F32),
        scratch_shapes=[
            pltpu.VMEM((d, T), BF16),
            pltpu.VMEM((2 * nh, nk, T), F32),
            pltpu.VMEM((2 * nh, PEER_TOPK, T), F32),
            pltpu.VMEM((nh, PEER_TOPK, T), F32),
            pltpu.VMEM((nh, SUBLANES, T), F32),
            pltpu.VMEM((nh, nk, T), F32),
            pltpu.VMEM((nh, nk, T), F32),
            pltpu.VMEM((nh, nk, T), F32),
            pltpu.VMEM((eb, T), F32),
            pltpu.VMEM((eb, T), F32),
            pltpu.VMEM((eb, T), BF16),
            pltpu.VMEM((eb, T), BF16),
            pltpu.VMEM((nh, SUBLANES, SUBLANES, T), F32),
            pltpu.VMEM((nh, SUBLANES, SUBLANES, T), F32),
            pltpu.VMEM((nh, SUBLANES, SUBLANES, T), F32),
            pltpu.VMEM((nh, SUBLANES, SUBLANES, T), F32),
            pltpu.VMEM((d, T), F32),
        ],
        compiler_params=_cparams("parallel", "arbitrary"),
        name="peer",
    )(x, sc, sh, g2, ng, wqT, keys, u_b, u_b, vT_b)


def _pick(n, pref):
    t = pref
    while n % t:
        t //= 2
    return t


def kernel(x, c, ctx, c_ctx, w_mod, b_mod, norm1_g, norm2_g, ab_w_in, a_ws, a_bs, a_vnorm_g, b_conv_w, ab_w_out,
           cd_w_in, c_short_w, c_short_b, c_filt_w1, c_filt_b1, c_filt_w2, c_filt_b2, c_filt_w3, c_filt_b3,
           c_filt_wout, c_filt_freq, c_skip, d_q_norm_g, d_k_norm_g, d_lambda_q1, d_lambda_k1, d_lambda_q2,
           d_lambda_k2, d_subln_g, cd_w_out, peer_wq, peer_keys, peer_u, peer_v):
    B, L, D = x.shape
    Lc = ctx.shape[1]
    depth = w_mod.shape[0]
    n_lat, n_ctx = B * L, B * Lc

    tm_l, tm_c = _pick(L, 256), _pick(Lc, 256)
    tq_l, tq_c = _pick(L, 256), _pick(Lc, 256)
    tp_l, tp_c = _pick(L, 512), _pick(n_ctx, 512)
    eb = 1024

    mod_rows = 2 * SUBLANES
    s_rows = jnp.zeros((mod_rows, D), F32).at[:B].set(c).at[B].set(c_ctx)
    mod = _mod_all(s_rows, w_mod, b_mod).reshape(depth, mod_rows, N_MOD, 1, D)

    xl = x.reshape(n_lat, D)
    xc = ctx.reshape(n_ctx, D)

    def lat_row(t):
        return lambda i: (i * t) // L

    ctx_row = lambda i: B

    col = jnp.arange(QK_W)
    inv = ROPE_BASE ** (-jnp.arange(0, ROPE_AXIS_DIM, 2, dtype=F32) / ROPE_AXIS_DIM)
    pos = jnp.arange(L)
    rows_f = (pos // GRID_W).astype(F32)
    cols_f = (pos % GRID_W).astype(F32)
    is_row_part = (col % D_QK) < ROPE_AXIS_DIM
    ang = jnp.where(is_row_part[None, :], rows_f[:, None], cols_f[:, None]) * inv[col % (ROPE_AXIS_DIM // 2)][None, :]
    cos_l = jnp.cos(ang)
    sin_l = jnp.where(((col % ROPE_AXIS_DIM) < ROPE_AXIS_DIM // 2)[None, :], -jnp.sin(ang), jnp.sin(ang))
    cos_c = jnp.ones((tm_c, QK_W), F32)
    sin_c = jnp.zeros((tm_c, QK_W), F32)

    deltas = jnp.abs(jnp.linspace(math.log(HY_TARGET) / HY_DECAY_LONG_PCT, math.log(HY_TARGET) / HY_DECAY_SHORT_PCT,
                                  C_WIDTH, dtype=F32))[None, :]

    for l in range(depth):
        last = l == depth - 1
        odd = l % 2 == 1
        i = l // 2
        m = lambda k: mod[l, :, k]
        sh1, sc1, g1, sh2, sc2, g2 = (m(k) for k in range(N_MOD))
        n1 = norm1_g[l][None, :]
        need_ctx = (not last) or odd

        if not odd:
            w_in = ab_w_in[i].astype(BF16)
            w_out = ab_w_out[i].astype(BF16)
            ws = a_ws[i].astype(BF16)
            bias_full = jnp.repeat(a_bs[i].T, A_HEAD_DIM, axis=1)
            vg = a_vnorm_g[i][None, :]
            cw = b_conv_w[i]
            p = _norm_mod_matmul(xl, sc1, sh1, n1, w_in, lat_row(tm_l), tm_l)
            xl = _mix_ab(p, xl, g1, ws, bias_full, vg, cw, w_out, lat_row(tm_l), L, tm_l)
            if not last:
                pc = _norm_mod_matmul(xc, sc1, sh1, n1, w_in, ctx_row, tm_c)
                xc = _mix_ab(pc, xc, g1, ws, bias_full, vg, cw, w_out, ctx_row, Lc, tm_c)
        else:
            lam_init = 0.8 - 0.6 * math.exp(-0.3 * l)
            w_in = cd_w_in[i].astype(BF16)
            w_out = cd_w_out[i].astype(BF16)
            qg = jnp.tile(d_q_norm_g[i], QK_W // D_QK)[None, :]
            kg = jnp.tile(d_k_norm_g[i], QK_W // D_QK)[None, :]
            lam_vecs = tuple(v[i][None, :] for v in (d_lambda_q1, d_lambda_k1, d_lambda_q2, d_lambda_k2))
            sg = d_subln_g[i][None, :]
            fw = LANES
            padc = lambda a: jnp.pad(a, ((0, 0), (0, fw - a.shape[1])))
            padr = lambda a: jnp.pad(a, ((0, fw - a.shape[0]), (0, 0)))
            fp = (padc(padr(c_filt_w1[i])), padc(c_filt_b1[i][None, :]), padc(padr(c_filt_w2[i])),
                  padc(c_filt_b2[i][None, :]), padc(padr(c_filt_w3[i])), padc(c_filt_b3[i][None, :]),
                  padr(c_filt_wout[i]), padc(c_filt_freq[i][None, :]))
            sw, sb, skip = c_short_w[i], c_short_b[i][None, :], c_skip[i][None, :]

            def hyena(pp, batch, seq, tm):
                x0, u, ub = _hy_pre(pp, sw, sb, seq, tm)
                fwd, invm = _dft_mats(seq)
                filt = _hy_filter(seq, fp, deltas)
                spec = _matmul(fwd, filt, _pick(2 * seq, 512))
                y = _hy_conv(ub, fwd, invm, spec, batch, seq, _pick(seq, 256))
                return x0, y, u

            pc = _norm_mod_matmul(xc, sc1, sh1, n1, w_in, ctx_row, tm_c)
            qmc, kc = _qk_prep(pc, cos_c, sin_c, qg, kg, lambda t: 0, tm_c)
            vc = pc[:, HY_IN + 2 * QK_W:].astype(BF16)
            p = _norm_mod_matmul(xl, sc1, sh1, n1, w_in, lat_row(tm_l), tm_l)
            qm, k = _qk_prep(p, cos_l, sin_l, qg, kg, lambda t: t % (L // tm_l), tm_l)
            v = p[:, HY_IN + 2 * QK_W:].astype(BF16)
            k_all = jnp.concatenate([kc.reshape(B, Lc, QK_W), k.reshape(B, L, QK_W)], axis=1)
            v_all = jnp.concatenate([vc.reshape(B, Lc, V_W), v.reshape(B, L, V_W)], axis=1)
            yd = _diff_attn(qm, k_all, v_all, lam_vecs, sg, lam_init, B, L, tq_l)
            x0, y, u = hyena(p, B, L, tm_l)
            xl = _proj_cd(x0, y, u, skip, yd, w_out, xl, g1, lat_row(tm_l), tm_l)
            if not last:
                ydc = _diff_attn(qmc, kc.reshape(B, Lc, QK_W), vc.reshape(B, Lc, V_W), lam_vecs, sg, lam_init,
                                 B, Lc, tq_c)
                x0c, yc, uc = hyena(pc, B, Lc, tm_c)
                xc = _proj_cd(x0c, yc, uc, skip, ydc, w_out, xc, g1, ctx_row, tm_c)

        n2 = norm2_g[l][None, :]
        wqT = peer_wq[l].T.astype(BF16)
        keys = peer_keys[l].reshape(2 * PEER_HEADS, peer_keys.shape[3], peer_keys.shape[4]).astype(BF16)
        u_b = peer_u[l].astype(BF16)
        vT_b = peer_v[l].T.astype(BF16)
        xl = _peer(xl, sc2, sh2, g2, n2, wqT, keys, u_b, vT_b, lat_row(tp_l), tp_l, eb)
        if not last:
            xc = _peer(xc, sc2, sh2, g2, n2, wqT, keys, u_b, vT_b, ctx_row, tp_c, eb)

    return xl.reshape(B, L, D)
```

```python
import functools
import math

import jax
import jax.numpy as jnp
from jax import lax
from jax.experimental import pallas as pl
from jax.experimental.pallas import tpu as pltpu

F32 = jnp.float32
BF16 = jnp.bfloat16
HI = lax.Precision.HIGHEST

GRID_W = 64
N_MOD = 6
RMS_EPS = 1e-6
CHUNK = 128
A_HEADS = 8
A_HEAD_DIM = 64
A_WIDTH = A_HEADS * A_HEAD_DIM
B_WIDTH = 512
C_WIDTH = 512
HY_IN = 3 * C_WIDTH
HY_EMB = 33
HY_DECAY_LONG_PCT = 1.5
HY_DECAY_SHORT_PCT = 0.3
HY_TARGET = 1e-2
D_HEADS = 4
D_QK = 64
D_V = 2 * D_QK
QK_W = D_HEADS * 2 * D_QK
V_W = D_HEADS * D_V
ATTN_SCALE = D_QK ** -0.5
ROPE_AXIS_DIM = D_QK // 2
ROPE_BASE = 10000.0
PEER_HEADS = 8
PEER_TOPK = 16

LANES = 128
SUBLANES = 8
VMEM_LIMIT = 56 * 1024 * 1024
GATE_ROWS = 32


def _cparams(*sem):
    return pltpu.CompilerParams(dimension_semantics=sem, vmem_limit_bytes=VMEM_LIMIT)


def _gelu(x):
    return 0.5 * x * (1.0 + lax.erf(x * (2.0 ** -0.5)))


def _oddeven_merge_sort_pairs(n):
    pairs, p = [], 1
    while p < n:
        k = p
        while k >= 1:
            for j in range(k % p, n - k, 2 * k):
                for i in range(min(k, n - j - k)):
                    if (i + j) // (2 * p) == (i + j + k) // (2 * p):
                        pairs.append((i + j, i + j + k))
            k //= 2
        p *= 2
    return pairs


def _bitonic_merge_pairs(n):
    pairs, d = [], n // 2
    while d >= 1:
        pairs += [(i, i + d) for i in range(n) if (i & d) == 0]
        d //= 2
    return pairs


_SORT_PAIRS = _oddeven_merge_sort_pairs(PEER_TOPK)
_MERGE_PAIRS = _bitonic_merge_pairs(PEER_TOPK)


def _group_mean_sq(v, group):
    w = v.shape[-1]
    r = lax.broadcasted_iota(jnp.int32, (w, w), 0) // group
    c = lax.broadcasted_iota(jnp.int32, (w, w), 1) // group
    bd = jnp.where(r == c, 1.0, 0.0).astype(BF16)
    vv = v * v
    hi = vv.astype(BF16)
    lo = (vv - hi.astype(F32)).astype(BF16)
    s = jnp.dot(hi, bd, preferred_element_type=F32) + jnp.dot(lo, bd, preferred_element_type=F32)
    return s * (1.0 / group)


def _mod_kernel(s_ref, w_ref, b_ref, o_ref):
    s = s_ref[...]
    s = s / (1.0 + jnp.exp(-s))
    o_ref[0] = jnp.dot(s, w_ref[0], precision=HI, preferred_element_type=F32) + b_ref[0]


def _mod_all(s_rows, w_mod, b_mod):
    depth, d, n6 = w_mod.shape
    rows = s_rows.shape[0]
    tn = n6 // 4
    return pl.pallas_call(
        _mod_kernel,
        grid=(depth, n6 // tn),
        in_specs=[
            pl.BlockSpec((rows, d), lambda l, j: (0, 0)),
            pl.BlockSpec((1, d, tn), lambda l, j: (l, 0, j)),
            pl.BlockSpec((1, 1, tn), lambda l, j: (l, 0, j)),
        ],
        out_specs=pl.BlockSpec((1, rows, tn), lambda l, j: (l, 0, j)),
        out_shape=jax.ShapeDtypeStruct((depth, rows, n6), F32),
        compiler_params=_cparams("arbitrary", "arbitrary"),
        name="mod",
    )(s_rows, w_mod, b_mod.reshape(depth, 1, n6))


def _norm_mod(x, g, sc, sh):
    ms = jnp.mean(x * x, axis=-1, keepdims=True)
    return (x * lax.rsqrt(ms + RMS_EPS) * g) * (1.0 + sc) + sh


def _nmm_kernel(x_ref, sc_ref, sh_ref, g_ref, w_ref, o_ref):
    h = _norm_mod(x_ref[...], g_ref[...], sc_ref[0], sh_ref[0])
    o_ref[...] = jnp.dot(h.astype(BF16), w_ref[...], preferred_element_type=F32)


def _norm_mod_matmul(x, sc, sh, g, w, row_of, tm):
    n, d = x.shape
    nout = w.shape[1]
    return pl.pallas_call(
        _nmm_kernel,
        grid=(n // tm,),
        in_specs=[
            pl.BlockSpec((tm, d), lambda i: (i, 0)),
            pl.BlockSpec((1, 1, d), lambda i: (row_of(i), 0, 0)),
            pl.BlockSpec((1, 1, d), lambda i: (row_of(i), 0, 0)),
            pl.BlockSpec((1, d), lambda i: (0, 0)),
            pl.BlockSpec((d, nout), lambda i: (0, 0)),
        ],
        out_specs=pl.BlockSpec((tm, nout), lambda i: (i, 0)),
        out_shape=jax.ShapeDtypeStruct((n, nout), F32),
        compiler_params=_cparams("parallel"),
        name="norm_mod_inproj",
    )(x, sc, sh, g, w)


def _shift_rows(z, prev_row, next_row):
    tm = z.shape[0]
    row = lax.broadcasted_iota(jnp.int32, z.shape, 0)
    up = jnp.where(row == 0, prev_row, pltpu.roll(z, 1, 0))
    dn = jnp.where(row == tm - 1, next_row, pltpu.roll(z, tm - 1, 0))
    return up, dn


def _mix_ab_kernel(p_ref, hbp_ref, gcp_ref, hbn_ref, gcn_ref, x_ref, g1_ref, ws_ref, bias_ref, vg_ref, cw_ref,
                   wout_ref, o_ref, *, tiles_per_seq):
    i = pl.program_id(0)
    tm = p_ref.shape[0]
    aw, bw = A_WIDTH, B_WIDTH
    u = _gelu(p_ref[:, 0:aw])
    v = _gelu(p_ref[:, aw:2 * aw])
    vn = (v * lax.rsqrt(_group_mean_sq(v, A_HEAD_DIM) + RMS_EPS) * vg_ref[...]).astype(BF16)
    sv_chunks = []
    for cidx in range(tm // CHUNK):
        vc = vn[cidx * CHUNK:(cidx + 1) * CHUNK]
        heads = [jnp.dot(ws_ref[h], vc[:, h * A_HEAD_DIM:(h + 1) * A_HEAD_DIM], preferred_element_type=F32)
                 for h in range(A_HEADS)]
        sv_chunks.append(jnp.concatenate(heads, axis=-1) + bias_ref[...])
    sv = jnp.concatenate(sv_chunks, axis=0) if len(sv_chunks) > 1 else sv_chunks[0]
    ya = u * sv

    o0 = 2 * aw
    hb = p_ref[:, o0:o0 + bw]
    gb = p_ref[:, o0 + bw:o0 + 2 * bw]
    gc = p_ref[:, o0 + 2 * bw:o0 + 3 * bw]
    z = gc * hb
    pos = i % tiles_per_seq
    zp = jnp.where(pos == 0, 0.0, gcp_ref[SUBLANES - 1:SUBLANES, :] * hbp_ref[SUBLANES - 1:SUBLANES, :])
    zn = jnp.where(pos == tiles_per_seq - 1, 0.0, gcn_ref[0:1, :] * hbn_ref[0:1, :])
    z_up, z_dn = _shift_rows(z, zp, zn)
    yb = gb * (cw_ref[0:1, :] * z_up + cw_ref[1:2, :] * z + cw_ref[2:3, :] * z_dn)

    y = jnp.concatenate([ya, yb], axis=-1).astype(BF16)
    o_ref[...] = x_ref[...] + g1_ref[0] * jnp.dot(y, wout_ref[...], preferred_element_type=F32)


def _mix_ab(p, x, g1, ws, bias_full, vg, cw, w_out, row_of, seq_len, tm):
    n, d = x.shape
    nin = p.shape[1]
    nb8 = n // SUBLANES
    r8 = tm // SUBLANES
    hb_blk = (2 * A_WIDTH) // B_WIDTH
    gc_blk = hb_blk + 2
    prev = lambda i: jnp.maximum(i * r8 - 1, 0)
    nxt = lambda i: jnp.minimum((i + 1) * r8, nb8 - 1)
    full = lambda a: pl.BlockSpec(a.shape, lambda i: (0,) * a.ndim)
    return pl.pallas_call(
        functools.partial(_mix_ab_kernel, tiles_per_seq=seq_len // tm),
        grid=(n // tm,),
        in_specs=[
            pl.BlockSpec((tm, nin), lambda i: (i, 0)),
            pl.BlockSpec((SUBLANES, B_WIDTH), lambda i: (prev(i), hb_blk)),
            pl.BlockSpec((SUBLANES, B_WIDTH), lambda i: (prev(i), gc_blk)),
            pl.BlockSpec((SUBLANES, B_WIDTH), lambda i: (nxt(i), hb_blk)),
            pl.BlockSpec((SUBLANES, B_WIDTH), lambda i: (nxt(i), gc_blk)),
            pl.BlockSpec((tm, d), lambda i: (i, 0)),
            pl.BlockSpec((1, 1, d), lambda i: (row_of(i), 0, 0)),
            full(ws), full(bias_full), full(vg), full(cw), full(w_out),
        ],
        out_specs=pl.BlockSpec((tm, d), lambda i: (i, 0)),
        out_shape=jax.ShapeDtypeStruct((n, d), F32),
        compiler_params=_cparams("parallel"),
        name="mix_ab",
    )(p, p, p, p, p, x, g1, ws, bias_full, vg, cw, w_out)


def _qk_prep_kernel(q_ref, k_ref, cos_ref, sin_ref, qg_ref, kg_ref, qm_ref, ko_ref):
    lane = lax.broadcasted_iota(jnp.int32, q_ref.shape, 1)
    first = (lane % ROPE_AXIS_DIM) < (ROPE_AXIS_DIM // 2)
    w = q_ref.shape[1]
    half = ROPE_AXIS_DIM // 2

    def prep(t, g):
        t = t * lax.rsqrt(_group_mean_sq(t, D_QK) + RMS_EPS) * g
        partner = jnp.where(first, pltpu.roll(t, w - half, 1), pltpu.roll(t, half, 1))
        return t * cos_ref[...] + partner * sin_ref[...]

    q = prep(q_ref[...], qg_ref[...]) * (ATTN_SCALE * math.log2(math.e))
    in_h0 = (lane % (2 * D_QK)) < D_QK
    qm_ref[0] = jnp.where(in_h0, q, 0.0).astype(BF16)
    qm_ref[1] = jnp.where(in_h0, 0.0, q).astype(BF16)
    ko_ref[...] = prep(k_ref[...], kg_ref[...]).astype(BF16)


def _qk_prep(p, cos_t, sin_t, qg, kg, tab_of, tm):
    n = p.shape[0]
    q_blk = HY_IN // QK_W
    return pl.pallas_call(
        _qk_prep_kernel,
        grid=(n // tm,),
        in_specs=[
            pl.BlockSpec((tm, QK_W), lambda i: (i, q_blk)),
            pl.BlockSpec((tm, QK_W), lambda i: (i, q_blk + 1)),
            pl.BlockSpec((tm, QK_W), lambda i: (tab_of(i), 0)),
            pl.BlockSpec((tm, QK_W), lambda i: (tab_of(i), 0)),
            pl.BlockSpec((1, QK_W), lambda i: (0, 0)),
            pl.BlockSpec((1, QK_W), lambda i: (0, 0)),
        ],
        out_specs=[
            pl.BlockSpec((2, tm, QK_W), lambda i: (0, i, 0)),
            pl.BlockSpec((tm, QK_W), lambda i: (i, 0)),
        ],
        out_shape=[jax.ShapeDtypeStruct((2, n, QK_W), BF16), jax.ShapeDtypeStruct((n, QK_W), BF16)],
        compiler_params=_cparams("parallel"),
        name="qk_prep",
    )(p, p, cos_t, sin_t, qg, kg)


def _diff_attn_kernel(q0_ref, q1_ref, k_ref, v_ref, lq1_ref, lk1_ref, lq2_ref, lk2_ref, sg_ref, o_ref, *, lam_init):
    lam = (jnp.exp(jnp.sum(lq1_ref[...] * lk1_ref[...], axis=-1, keepdims=True))
           - jnp.exp(jnp.sum(lq2_ref[...] * lk2_ref[...], axis=-1, keepdims=True)) + lam_init)
    k = k_ref[0]
    v = v_ref[0]

    def half(q):
        s = lax.dot_general(q, k, (((1,), (1,)), ((), ())), preferred_element_type=F32)
        m = jnp.max(s, axis=-1, keepdims=True)
        p = jnp.exp2(s - m)
        l = jnp.sum(p, axis=-1, keepdims=True)
        return jnp.dot(p.astype(BF16), v, preferred_element_type=F32) / l

    o = half(q0_ref[0]) - lam * half(q1_ref[0])
    ms = jnp.mean(o * o, axis=-1, keepdims=True)
    o_ref[...] = (o * lax.rsqrt(ms + RMS_EPS) * sg_ref[...]) * (1.0 - lam_init)


def _diff_attn(qm, k_all, v_all, lam_vecs, subln_g, lam_init, batch, lq, tq):
    lk = k_all.shape[1]
    nq = lq // tq
    vec = lambda a: pl.BlockSpec(a.shape, lambda b, h, i: (0, 0))
    return pl.pallas_call(
        functools.partial(_diff_attn_kernel, lam_init=lam_init),
        grid=(batch, D_HEADS, nq),
        in_specs=[
            pl.BlockSpec((1, tq, D_V), lambda b, h, i: (0, b * nq + i, h)),
            pl.BlockSpec((1, tq, D_V), lambda b, h, i: (1, b * nq + i, h)),
            pl.BlockSpec((1, lk, D_V), lambda b, h, i: (b, 0, h)),
            pl.BlockSpec((1, lk, D_V), lambda b, h, i: (b, 0, h)),
            vec(lam_vecs[0]), vec(lam_vecs[1]), vec(lam_vecs[2]), vec(lam_vecs[3]), vec(subln_g),
        ],
        out_specs=pl.BlockSpec((tq, D_V), lambda b, h, i: (b * nq + i, h)),
        out_shape=jax.ShapeDtypeStruct((batch * lq, V_W), F32),
        compiler_params=_cparams("parallel", "parallel", "arbitrary"),
        name="diff_attn",
    )(qm, qm, k_all, v_all, *lam_vecs, subln_g)


def _hy_pre_kernel(p_ref, pp_ref, pn_ref, sw_ref, sb_ref, x0_ref, u_ref, ub_ref, *, tiles_per_seq):
    i = pl.program_id(0)
    pos = i % tiles_per_seq
    z = p_ref[...]
    zp = jnp.where(pos == 0, 0.0, pp_ref[SUBLANES - 1:SUBLANES, :])
    zn = jnp.where(pos == tiles_per_seq - 1, 0.0, pn_ref[0:1, :])
    z_up, z_dn = _shift_rows(z, zp, zn)
    c = sw_ref[0:1, :] * z_up + sw_ref[1:2, :] * z + sw_ref[2:3, :] * z_dn + sb_ref[...]
    cw = C_WIDTH
    u = c[:, cw:2 * cw] * c[:, 2 * cw:3 * cw]
    x0_ref[...] = c[:, 0:cw]
    u_ref[...] = u
    ub_ref[...] = u.astype(BF16)


def _hy_pre(p, sw, sb, seq_len, tm):
    n = p.shape[0]
    nb8 = n // SUBLANES
    r8 = tm // SUBLANES
    prev = lambda i: jnp.maximum(i * r8 - 1, 0)
    nxt = lambda i: jnp.minimum((i + 1) * r8, nb8 - 1)
    o_spec = pl.BlockSpec((tm, C_WIDTH), lambda i: (i, 0))
    return pl.pallas_call(
        functools.partial(_hy_pre_kernel, tiles_per_seq=seq_len // tm),
        grid=(n // tm,),
        in_specs=[
            pl.BlockSpec((tm, HY_IN), lambda i: (i, 0)),
            pl.BlockSpec((SUBLANES, HY_IN), lambda i: (prev(i), 0)),
            pl.BlockSpec((SUBLANES, HY_IN), lambda i: (nxt(i), 0)),
            pl.BlockSpec((3, HY_IN), lambda i: (0, 0)),
            pl.BlockSpec((1, HY_IN), lambda i: (0, 0)),
        ],
        out_specs=[o_spec, o_spec, o_spec],
        out_shape=[jax.ShapeDtypeStruct((n, C_WIDTH), F32), jax.ShapeDtypeStruct((n, C_WIDTH), F32),
                   jax.ShapeDtypeStruct((n, C_WIDTH), BF16)],
        compiler_params=_cparams("parallel"),
        name="hyena_pre",
    )(p, p, p, sw, sb)


def _hy_filter_kernel(w1_ref, b1_ref, w2_ref, b2_ref, w3_ref, b3_ref, wo_ref, fr_ref, dl_ref, o_ref):
    L = o_ref.shape[0]
    fw = w1_ref.shape[0]
    bands = (HY_EMB - 1) // 2
    row = lax.broadcasted_iota(jnp.int32, (L, fw), 0).astype(F32)
    col = lax.broadcasted_iota(jnp.int32, (L, fw), 1)
    t = row / (L - 1.0)
    ang = (2.0 * math.pi) * row / L
    step = (bands - 1 - 1e-4) / (bands - 1)
    band = jnp.where(col <= bands, col - 1, col - 1 - bands).astype(F32)
    arg = (1e-4 + band * step) * ang
    feats = jnp.where(col == 0, t, jnp.where(col <= bands, jnp.cos(arg), jnp.where(col < HY_EMB, -jnp.sin(arg), 0.0)))
    fr = fr_ref[...]
    mm = lambda a, w: jnp.dot(a, w, precision=HI, preferred_element_type=F32)
    a = jnp.sin(fr * (mm(feats, w1_ref[...]) + b1_ref[...]))
    a = jnp.sin(fr * (mm(a, w2_ref[...]) + b2_ref[...]))
    a = jnp.sin(fr * (mm(a, w3_ref[...]) + b3_ref[...]))
    hf = mm(a, wo_ref[...])
    rowc = lax.broadcasted_iota(jnp.int32, (L, C_WIDTH), 0)
    tc = rowc.astype(F32) / (L - 1.0)
    decay = jnp.exp(-tc * dl_ref[...])
    h_fwd = hf[:, :C_WIDTH] * decay
    h_bwd = jnp.where(rowc == 0, 0.0, hf[:, C_WIDTH:] * decay)
    norm = jnp.sum(jnp.abs(h_fwd) + jnp.abs(h_bwd), axis=0, keepdims=True)
    o_ref[:, :C_WIDTH] = (h_fwd / norm).astype(BF16)
    o_ref[:, C_WIDTH:] = (h_bwd / norm).astype(BF16)


def _hy_filter(L, fp, deltas):
    full = lambda a: pl.BlockSpec(a.shape, lambda: (0,) * a.ndim)
    args = (*fp, deltas)
    return pl.pallas_call(
        _hy_filter_kernel,
        in_specs=[full(a) for a in args],
        out_specs=pl.BlockSpec((L, 2 * C_WIDTH), lambda: (0, 0)),
        out_shape=jax.ShapeDtypeStruct((L, 2 * C_WIDTH), BF16),
        compiler_params=pltpu.CompilerParams(vmem_limit_bytes=VMEM_LIMIT),
        name="hyena_filter",
    )(*args)


def _mm_kernel(a_ref, b_ref, o_ref):
    o_ref[...] = jnp.dot(a_ref[...], b_ref[...], preferred_element_type=F32)


def _matmul(a, b, tm):
    m, k = a.shape
    n = b.shape[1]
    return pl.pallas_call(
        _mm_kernel,
        grid=(m // tm,),
        in_specs=[pl.BlockSpec((tm, k), lambda i: (i, 0)), pl.BlockSpec((k, n), lambda i: (0, 0))],
        out_specs=pl.BlockSpec((tm, n), lambda i: (i, 0)),
        out_shape=jax.ShapeDtypeStruct((m, n), F32),
        compiler_params=_cparams("parallel"),
        name="dft_filter",
    )(a, b)


def _hy_conv_kernel(u_ref, wt_ref, wb_ref, w2t_ref, w2b_ref, at_ref, ab_ref, y_ref):
    j = pl.program_id(1)
    cw = C_WIDTH
    u = u_ref[...]
    uc = jnp.dot(wt_ref[...], u, preferred_element_type=F32)
    us = jnp.dot(wb_ref[...], u, preferred_element_type=F32)
    kre = at_ref[:, :cw] + at_ref[:, cw:]
    kim = ab_ref[:, cw:] - ab_ref[:, :cw]
    knyq = ab_ref[:, :cw] + ab_ref[:, cw:]
    row = lax.broadcasted_iota(jnp.int32, kre.shape, 0)
    special = jnp.logical_and(row == 0, j == 0)
    p2 = jnp.where(special, 0.0, kim)
    p4 = jnp.where(special, knyq, -kre)
    yt = (uc * kre + us * p2).astype(BF16)
    yb = (uc * p2 + us * p4).astype(BF16)
    L = y_ref.shape[0]
    tr = min(L, 512)

    def inverse_rows(rows, first):
        contrib = (jnp.dot(w2t_ref[rows, :], yt, preferred_element_type=F32)
                   + jnp.dot(w2b_ref[rows, :], yb, preferred_element_type=F32))
        y_ref[rows, :] = contrib if first else y_ref[rows, :] + contrib

    @pl.when(j == 0)
    def _():
        for c in range(L // tr):
            inverse_rows(slice(c * tr, (c + 1) * tr), True)

    @pl.when(j > 0)
    def _():
        for c in range(L // tr):
            inverse_rows(slice(c * tr, (c + 1) * tr), False)


def _hy_conv(ub, w, w2, a_spec, batch, L, rb):
    ns = L // rb
    return pl.pallas_call(
        _hy_conv_kernel,
        grid=(batch, ns),
        in_specs=[
            pl.BlockSpec((L, C_WIDTH), lambda b, j: (b, 0)),
            pl.BlockSpec((rb, L), lambda b, j: (j, 0)),
            pl.BlockSpec((rb, L), lambda b, j: (j + ns, 0)),
            pl.BlockSpec((L, rb), lambda b, j: (0, j)),
            pl.BlockSpec((L, rb), lambda b, j: (0, j + ns)),
            pl.BlockSpec((rb, 2 * C_WIDTH), lambda b, j: (j, 0)),
            pl.BlockSpec((rb, 2 * C_WIDTH), lambda b, j: (j + ns, 0)),
        ],
        out_specs=pl.BlockSpec((L, C_WIDTH), lambda b, j: (b, 0)),
        out_shape=jax.ShapeDtypeStruct((batch * L, C_WIDTH), F32),
        compiler_params=_cparams("parallel", "arbitrary"),
        name="hyena_conv",
    )(ub, w, w, w2, w2, a_spec, a_spec)


def _proj_cd_kernel(x0_ref, y_ref, u_ref, skip_ref, yd_ref, w_ref, x_ref, g1_ref, o_ref):
    yh = (x0_ref[...] * (y_ref[...] + u_ref[...] * skip_ref[...])).astype(BF16)
    acc = jnp.dot(yh, w_ref[:C_WIDTH, :], preferred_element_type=F32)
    acc += jnp.dot(yd_ref[...].astype(BF16), w_ref[C_WIDTH:, :], preferred_element_type=F32)
    o_ref[...] = x_ref[...] + g1_ref[0] * acc


def _proj_cd(x0, y, u, skip, yd, w_out, x, g1, row_of, tm):
    n, d = x.shape
    half = pl.BlockSpec((tm, C_WIDTH), lambda i: (i, 0))
    return pl.pallas_call(
        _proj_cd_kernel,
        grid=(n // tm,),
        in_specs=[
            half, half, half,
            pl.BlockSpec((1, C_WIDTH), lambda i: (0, 0)),
            half,
            pl.BlockSpec(w_out.shape, lambda i: (0, 0)),
            pl.BlockSpec((tm, d), lambda i: (i, 0)),
            pl.BlockSpec((1, 1, d), lambda i: (row_of(i), 0, 0)),
        ],
        out_specs=pl.BlockSpec((tm, d), lambda i: (i, 0)),
        out_shape=jax.ShapeDtypeStruct((n, d), F32),
        compiler_params=_cparams("parallel"),
        name="proj_cd",
    )(x0, y, u, skip, yd, w_out, x, g1)


def _dft_mats(L):
    m = 2 * L
    r = jnp.arange(L, dtype=jnp.int32)
    blk = math.gcd(L, 64)
    phase = lambda cols: ((r[:, None] * cols[None, :]) % m).astype(F32) * (2.0 * math.pi / m)
    a_hi = phase(jnp.arange(L // blk, dtype=jnp.int32) * blk)[:, :, None]
    a_lo = phase(jnp.arange(blk, dtype=jnp.int32))[:, None, :]
    c = (jnp.cos(a_hi) * jnp.cos(a_lo) - jnp.sin(a_hi) * jnp.sin(a_lo)).reshape(L, L)
    s = (jnp.sin(a_hi) * jnp.cos(a_lo) + jnp.cos(a_hi) * jnp.sin(a_lo)).reshape(L, L)
    alt = jnp.where(r % 2 == 0, 1.0, -1.0).astype(F32)
    is0 = (r == 0)[:, None]
    fwd = jnp.concatenate([c, jnp.where(is0, alt[None, :], s)], axis=0)
    inv_top = jnp.where(is0, 1.0 / m, c * (2.0 / m))
    inv_bot = jnp.where(is0, alt[None, :] / m, s * (-2.0 / m))
    inv = jnp.concatenate([inv_top, inv_bot], axis=0).T
    return fwd.astype(BF16), inv.astype(BF16)


def _peer_kernel(x_ref, sc_ref, sh_ref, g2_ref, ng_ref, wqT_ref, keys_ref, u0_ref, u_ref, vT_ref, o_ref,
                 h2T_s, sT_s, tops_s, th_s, iz_s, aT_s, bT_s, thT_s, act0_s, act1_s, w0_s, w1_s,
                 cutb0_s, cutb1_s, ab0_s, ab1_s, acc_s, *, nblk):
    e = pl.program_id(1)
    nk = keys_ref.shape[1]
    T = x_ref.shape[0]
    eb = u_ref.shape[0]
    nh = PEER_HEADS
    ntc = T // LANES
    neg = -jnp.inf

    def fill_rows(cutb, ab, blk):
        base = blk * SUBLANES if isinstance(blk, int) else pl.multiple_of(blk * SUBLANES, SUBLANES)
        for h in range(nh):
            cut8 = thT_s[h, pl.ds(base, SUBLANES), :]
            a8 = aT_s[h, pl.ds(base, SUBLANES), :]
            for ii in range(SUBLANES):
                cutb[h, ii] = jnp.broadcast_to(cut8[ii:ii + 1, :], (SUBLANES, T))
                ab[h, ii] = jnp.broadcast_to(a8[ii:ii + 1, :], (SUBLANES, T))

    @pl.when(e == 0)
    def _route():
        h2 = _norm_mod(x_ref[...], ng_ref[...], sc_ref[0], sh_ref[0])
        h2T = h2.T.astype(BF16)
        h2T_s[...] = h2T
        qT = jnp.dot(wqT_ref[...], h2T, preferred_element_type=F32).astype(BF16)
        pk = keys_ref.shape[2]
        for hp in range(2 * nh):
            sT_s[hp] = jnp.dot(keys_ref[hp], qT[hp * pk:(hp + 1) * pk, :], preferred_element_type=F32)

        def exchange(v, pairs):
            for i, j in pairs:
                v[i], v[j] = jnp.maximum(v[i], v[j]), jnp.minimum(v[i], v[j])

        def top_body(hp, carry):
            for tc in range(ntc):
                sl = slice(tc * LANES, (tc + 1) * LANES)
                v = [sT_s[hp, i * SUBLANES:(i + 1) * SUBLANES, sl] for i in range(PEER_TOPK)]
                exchange(v, _SORT_PAIRS)
                shift = SUBLANES // 2
                while shift >= 1:
                    w = [pltpu.roll(a, shift, 0) for a in v]
                    v = [jnp.maximum(v[i], w[PEER_TOPK - 1 - i]) for i in range(PEER_TOPK)]
                    exchange(v, _MERGE_PAIRS)
                    shift //= 2
                for i in range(PEER_TOPK):
                    tops_s[hp, i:i + 1, sl] = v[i][0:1]
            return carry

        lax.fori_loop(0, 2 * nh, top_body, 0)

        act0_s[...] = jnp.dot(u0_ref[...], h2T, preferred_element_type=F32)

        pairs = [(i, j) for i in range(PEER_TOPK) for j in range(PEER_TOPK // (i + 1))]
        for tc in range(ntc):
            sl = slice(tc * LANES, (tc + 1) * LANES)
            A = [jnp.concatenate([tops_s[2 * h, i:i + 1, sl] for h in range(nh)], axis=0) for i in range(PEER_TOPK)]
            B = [jnp.concatenate([tops_s[2 * h + 1, j:j + 1, sl] for h in range(nh)], axis=0)
                 for j in range(PEER_TOPK)]
            cand = [A[i] + B[j] for (i, j) in pairs]
            work = list(cand)
            tau = None
            for r in range(PEER_TOPK):
                tau = functools.reduce(jnp.maximum, work)
                if r < PEER_TOPK - 1:
                    work = [jnp.where(c >= tau, neg, c) for c in work]
            m0 = cand[0]
            z = functools.reduce(lambda a, b: a + b, [jnp.where(c >= tau, jnp.exp(c - m0), 0.0) for c in cand])
            iz = 1.0 / z
            for h in range(nh):
                iz_s[h, 0:1, sl] = iz[h:h + 1]
            for i in range(PEER_TOPK):
                row = [jnp.where(cand[n] >= tau, B[j], jnp.inf) for n, (ii, j) in enumerate(pairs) if ii == i]
                cut = functools.reduce(jnp.minimum, row)
                for h in range(nh):
                    th_s[h, i:i + 1, sl] = cut[h:h + 1]

        def head_body(h, carry):
            s0 = sT_s[2 * h]
            s1 = sT_s[2 * h + 1]
            a0 = tops_s[2 * h, 0:1, :]
            b0 = tops_s[2 * h + 1, 0:1, :]
            aT_s[h] = jnp.exp(s0 - a0) * (0.5 * iz_s[h, 0:1, :])
            bT_s[h] = jnp.exp(s1 - b0)
            th = jnp.full(s0.shape, jnp.inf, F32)
            for i in range(PEER_TOPK):
                th = jnp.where(s0 == tops_s[2 * h, i:i + 1, :], th_s[h, i:i + 1, :], th)
            thT_s[h] = th
            return carry

        lax.fori_loop(0, nh, head_body, 0)

        w1_s[...] = jnp.zeros(w1_s.shape, BF16)
        acc_s[...] = jnp.zeros(acc_s.shape, F32)
        fill_rows(cutb0_s, ab0_s, 0)

    def stage(act_w, act_r, w_w, w_r, cutb_w, ab_w, cutb_r, ab_r):
        nrc = nk // GATE_ROWS
        nvr = GATE_ROWS // SUBLANES
        mrows = eb // ntc
        drows = acc_s.shape[0] // ntc
        def mxu_slice(tc):
            ru = slice(tc * mrows, (tc + 1) * mrows)
            act_w[ru, :] = jnp.dot(u_ref[ru, :], h2T_s[...], preferred_element_type=F32)
            rd = slice(tc * drows, (tc + 1) * drows)
            acc_s[rd, :] += jnp.dot(vT_ref[rd, :], w_r[...], preferred_element_type=F32)

        for tc in range(ntc):
            sl = slice(tc * LANES, (tc + 1) * LANES)
            for rc in range(nrc):
                g = [[jnp.zeros((SUBLANES, LANES), F32) for _ in range(nvr)] for _ in range(SUBLANES)]
                for h in range(nh):
                    s1c = [sT_s[2 * h + 1, rc * GATE_ROWS + v * SUBLANES:rc * GATE_ROWS + (v + 1) * SUBLANES, sl]
                           for v in range(nvr)]
                    bc = [bT_s[h, rc * GATE_ROWS + v * SUBLANES:rc * GATE_ROWS + (v + 1) * SUBLANES, sl]
                          for v in range(nvr)]
                    for ii in range(SUBLANES):
                        cut = cutb_r[h, ii, :, sl]
                        a = ab_r[h, ii, :, sl]
                        for v in range(nvr):
                            g[ii][v] = g[ii][v] + jnp.where(s1c[v] >= cut, a * bc[v], 0.0)
                for ii in range(SUBLANES):
                    rows = slice(ii * nk + rc * GATE_ROWS, ii * nk + (rc + 1) * GATE_ROWS)
                    gi = jnp.concatenate(g[ii], axis=0)
                    act = act_r[rows, sl]
                    w_w[rows, sl] = (gi * (act * (1.0 + lax.erf(act * (2.0 ** -0.5))))).astype(BF16)
            mxu_slice(tc)
        fill_rows(cutb_w, ab_w, jnp.minimum(e + 1, nblk - 1))

    @pl.when(jnp.logical_and(e < nblk, e % 2 == 0))
    def _():
        stage(act1_s, act0_s, w0_s, w1_s, cutb1_s, ab1_s, cutb0_s, ab0_s)

    @pl.when(jnp.logical_and(e < nblk, e % 2 == 1))
    def _():
        stage(act0_s, act1_s, w1_s, w0_s, cutb0_s, ab0_s, cutb1_s, ab1_s)

    @pl.when(e == nblk)
    def _():
        w_last = w0_s if (nblk - 1) % 2 == 0 else w1_s
        acc = acc_s[...] + jnp.dot(vT_ref[...], w_last[...], preferred_element_type=F32)
        o_ref[...] = x_ref[...] + g2_ref[0] * acc.T


def _peer(x, sc, sh, g2, ng, wqT, keys, u_b, vT_b, row_of, T, eb):
    n, d = x.shape
    n_exp = u_b.shape[0]
    nk = keys.shape[1]
    nh = PEER_HEADS
    modspec = pl.BlockSpec((1, 1, d), lambda i, e: (row_of(i), 0, 0))
    nblk = n_exp // eb
    assert eb == nk * SUBLANES and nk % GATE_ROWS == 0 and T % LANES == 0 and nk == SUBLANES * PEER_TOPK
    once = pl.Buffered(1)
    return pl.pallas_call(
        functools.partial(_peer_kernel, nblk=nblk),
        grid=(n // T, nblk + 1),
        in_specs=[
            pl.BlockSpec((T, d), lambda i, e: (i, 0)),
            modspec, modspec, modspec,
            pl.BlockSpec((1, d), lambda i, e: (0, 0)),
            pl.BlockSpec(wqT.shape, lambda i, e: (0, 0), pipeline_mode=once),
            pl.BlockSpec(keys.shape, lambda i, e: (0, 0, 0), pipeline_mode=once),
            pl.BlockSpec((eb, d), lambda i, e: (0, 0), pipeline_mode=once),
            pl.BlockSpec((eb, d), lambda i, e: (jnp.minimum(e + 1, nblk - 1), 0)),
            pl.BlockSpec((d, eb), lambda i, e: (0, jnp.clip(e - 1, 0, nblk - 1))),
        ],
        out_specs=pl.BlockSpec((T, d), lambda i, e: (i, 0)),
        out_shape=jax.ShapeDtypeStruct((n, d), F32),
        scratch_shapes=[
            pltpu.VMEM((d, T), BF16),
            pltpu.VMEM((2 * nh, nk, T), F32),
            pltpu.VMEM((2 * nh, PEER_TOPK, T), F32),
            pltpu.VMEM((nh, PEER_TOPK, T), F32),
            pltpu.VMEM((nh, SUBLANES, T), F32),
            pltpu.VMEM((nh, nk, T), F32),
            pltpu.VMEM((nh, nk, T), F32),
            pltpu.VMEM((nh, nk, T), F32),
            pltpu.VMEM((eb, T), F32),
            pltpu.VMEM((eb, T), F32),
            pltpu.VMEM((eb, T), BF16),
            pltpu.VMEM((eb, T), BF16),
            pltpu.VMEM((nh, SUBLANES, SUBLANES, T), F32),
            pltpu.VMEM((nh, SUBLANES, SUBLANES, T), F32),
            pltpu.VMEM((nh, SUBLANES, SUBLANES, T), F32),
            pltpu.VMEM((nh, SUBLANES, SUBLANES, T), F32),
            pltpu.VMEM((d, T), F32),
        ],
        compiler_params=_cparams("parallel", "arbitrary"),
        name="peer",
    )(x, sc, sh, g2, ng, wqT, keys, u_b, u_b, vT_b)


def _pick(n, pref):
    t = pref
    while n % t:
        t //= 2
    return t


def kernel(x, c, ctx, c_ctx, w_mod, b_mod, norm1_g, norm2_g, ab_w_in, a_ws, a_bs, a_vnorm_g, b_conv_w, ab_w_out,
           cd_w_in, c_short_w, c_short_b, c_filt_w1, c_filt_b1, c_filt_w2, c_filt_b2, c_filt_w3, c_filt_b3,
           c_filt_wout, c_filt_freq, c_skip, d_q_norm_g, d_k_norm_g, d_lambda_q1, d_lambda_k1, d_lambda_q2,
           d_lambda_k2, d_subln_g, cd_w_out, peer_wq, peer_keys, peer_u, peer_v):
    B, L, D = x.shape
    Lc = ctx.shape[1]
    depth = w_mod.shape[0]
    n_lat, n_ctx = B * L, B * Lc

    tm_l, tm_c = _pick(L, 256), _pick(Lc, 256)
    tq_l, tq_c = _pick(L, 256), _pick(Lc, 256)
    tp_l, tp_c = _pick(L, 512), _pick(n_ctx, 512)
    eb = 1024

    mod_rows = 2 * SUBLANES
    s_rows = jnp.zeros((mod_rows, D), F32).at[:B].set(c).at[B].set(c_ctx)
    mod = _mod_all(s_rows, w_mod, b_mod).reshape(depth, mod_rows, N_MOD, 1, D)

    xl = x.reshape(n_lat, D)
    xc = ctx.reshape(n_ctx, D)

    def lat_row(t):
        return lambda i: (i * t) // L

    ctx_row = lambda i: B

    col = jnp.arange(QK_W)
    inv = ROPE_BASE ** (-jnp.arange(0, ROPE_AXIS_DIM, 2, dtype=F32) / ROPE_AXIS_DIM)
    pos = jnp.arange(L)
    rows_f = (pos // GRID_W).astype(F32)
    cols_f = (pos % GRID_W).astype(F32)
    is_row_part = (col % D_QK) < ROPE_AXIS_DIM
    ang = jnp.where(is_row_part[None, :], rows_f[:, None], cols_f[:, None]) * inv[col % (ROPE_AXIS_DIM // 2)][None, :]
    cos_l = jnp.cos(ang)
    sin_l = jnp.where(((col % ROPE_AXIS_DIM) < ROPE_AXIS_DIM // 2)[None, :], -jnp.sin(ang), jnp.sin(ang))
    cos_c = jnp.ones((tm_c, QK_W), F32)
    sin_c = jnp.zeros((tm_c, QK_W), F32)

    deltas = jnp.abs(jnp.linspace(math.log(HY_TARGET) / HY_DECAY_LONG_PCT, math.log(HY_TARGET) / HY_DECAY_SHORT_PCT,
                                  C_WIDTH, dtype=F32))[None, :]

    for l in range(depth):
        last = l == depth - 1
        odd = l % 2 == 1
        i = l // 2
        m = lambda k: mod[l, :, k]
        sh1, sc1, g1, sh2, sc2, g2 = (m(k) for k in range(N_MOD))
        n1 = norm1_g[l][None, :]
        need_ctx = (not last) or odd

        if not odd:
            w_in = ab_w_in[i].astype(BF16)
            w_out = ab_w_out[i].astype(BF16)
            ws = a_ws[i].astype(BF16)
            bias_full = jnp.repeat(a_bs[i].T, A_HEAD_DIM, axis=1)
            vg = a_vnorm_g[i][None, :]
            cw = b_conv_w[i]
            p = _norm_mod_matmul(xl, sc1, sh1, n1, w_in, lat_row(tm_l), tm_l)
            xl = _mix_ab(p, xl, g1, ws, bias_full, vg, cw, w_out, lat_row(tm_l), L, tm_l)
            if not last:
                pc = _norm_mod_matmul(xc, sc1, sh1, n1, w_in, ctx_row, tm_c)
                xc = _mix_ab(pc, xc, g1, ws, bias_full, vg, cw, w_out, ctx_row, Lc, tm_c)
        else:
            lam_init = 0.8 - 0.6 * math.exp(-0.3 * l)
            w_in = cd_w_in[i].astype(BF16)
            w_out = cd_w_out[i].astype(BF16)
            qg = jnp.tile(d_q_norm_g[i], QK_W // D_QK)[None, :]
            kg = jnp.tile(d_k_norm_g[i], QK_W // D_QK)[None, :]
            lam_vecs = tuple(v[i][None, :] for v in (d_lambda_q1, d_lambda_k1, d_lambda_q2, d_lambda_k2))
            sg = d_subln_g[i][None, :]
            fw = LANES
            padc = lambda a: jnp.pad(a, ((0, 0), (0, fw - a.shape[1])))
            padr = lambda a: jnp.pad(a, ((0, fw - a.shape[0]), (0, 0)))
            fp = (padc(padr(c_filt_w1[i])), padc(c_filt_b1[i][None, :]), padc(padr(c_filt_w2[i])),
                  padc(c_filt_b2[i][None, :]), padc(padr(c_filt_w3[i])), padc(c_filt_b3[i][None, :]),
                  padr(c_filt_wout[i]), padc(c_filt_freq[i][None, :]))
            sw, sb, skip = c_short_w[i], c_short_b[i][None, :], c_skip[i][None, :]

            def hyena(pp, batch, seq, tm):
                x0, u, ub = _hy_pre(pp, sw, sb, seq, tm)
                fwd, invm = _dft_mats(seq)
                filt = _hy_filter(seq, fp, deltas)
                spec = _matmul(fwd, filt, _pick(2 * seq, 512))
                y = _hy_conv(ub, fwd, invm, spec, batch, seq, _pick(seq, 256))
                return x0, y, u

            pc = _norm_mod_matmul(xc, sc1, sh1, n1, w_in, ctx_row, tm_c)
            qmc, kc = _qk_prep(pc, cos_c, sin_c, qg, kg, lambda t: 0, tm_c)
            vc = pc[:, HY_IN + 2 * QK_W:].astype(BF16)
            p = _norm_mod_matmul(xl, sc1, sh1, n1, w_in, lat_row(tm_l), tm_l)
            qm, k = _qk_prep(p, cos_l, sin_l, qg, kg, lambda t: t % (L // tm_l), tm_l)
            v = p[:, HY_IN + 2 * QK_W:].astype(BF16)
            k_all = jnp.concatenate([kc.reshape(B, Lc, QK_W), k.reshape(B, L, QK_W)], axis=1)
            v_all = jnp.concatenate([vc.reshape(B, Lc, V_W), v.reshape(B, L, V_W)], axis=1)
            yd = _diff_attn(qm, k_all, v_all, lam_vecs, sg, lam_init, B, L, tq_l)
            x0, y, u = hyena(p, B, L, tm_l)
            xl = _proj_cd(x0, y, u, skip, yd, w_out, xl, g1, lat_row(tm_l), tm_l)
            if not last:
                ydc = _diff_attn(qmc, kc.reshape(B, Lc, QK_W), vc.reshape(B, Lc, V_W), lam_vecs, sg, lam_init,
                                 B, Lc, tq_c)
                x0c, yc, uc = hyena(pc, B, Lc, tm_c)
                xc = _proj_cd(x0c, yc, uc, skip, ydc, w_out, xc, g1, ctx_row, tm_c)

        n2 = norm2_g[l][None, :]
        wqT = peer_wq[l].T.astype(BF16)
        keys = peer_keys[l].reshape(2 * PEER_HEADS, peer_keys.shape[3], peer_keys.shape[4]).astype(BF16)
        u_b = peer_u[l].astype(BF16)
        vT_b = peer_v[l].T.astype(BF16)
        xl = _peer(xl, sc2, sh2, g2, n2, wqT, keys, u_b, vT_b, lat_row(tp_l), tp_l, eb)
        if not last:
            xc = _peer(xc, sc2, sh2, g2, n2, wqT, keys, u_b, vT_b, ctx_row, tp_c, eb)

    return xl.reshape(B, L, D)
```

```python
import functools
import math

import jax
import jax.numpy as jnp
from jax import lax
from jax.experimental import pallas as pl
from jax.experimental.pallas import tpu as pltpu

F32 = jnp.float32
BF16 = jnp.bfloat16
HI = lax.Precision.HIGHEST

GRID_W = 64
N_MOD = 6
RMS_EPS = 1e-6
CHUNK = 128
A_HEADS = 8
A_HEAD_DIM = 64
A_WIDTH = A_HEADS * A_HEAD_DIM
B_WIDTH = 512
C_WIDTH = 512
HY_IN = 3 * C_WIDTH
HY_EMB = 33
HY_DECAY_LONG_PCT = 1.5
HY_DECAY_SHORT_PCT = 0.3
HY_TARGET = 1e-2
D_HEADS = 4
D_QK = 64
D_V = 2 * D_QK
QK_W = D_HEADS * 2 * D_QK
V_W = D_HEADS * D_V
ATTN_SCALE = D_QK ** -0.5
ROPE_AXIS_DIM = D_QK // 2
ROPE_BASE = 10000.0
PEER_HEADS = 8
PEER_TOPK = 16

LANES = 128
SUBLANES = 8
VMEM_LIMIT = 56 * 1024 * 1024
GATE_ROWS = 32


def _cparams(*sem):
    return pltpu.CompilerParams(dimension_semantics=sem, vmem_limit_bytes=VMEM_LIMIT)


def _gelu(x):
    return 0.5 * x * (1.0 + lax.erf(x * (2.0 ** -0.5)))


def _oddeven_merge_sort_pairs(n):
    pairs, p = [], 1
    while p < n:
        k = p
        while k >= 1:
            for j in range(k % p, n - k, 2 * k):
                for i in range(min(k, n - j - k)):
                    if (i + j) // (2 * p) == (i + j + k) // (2 * p):
                        pairs.append((i + j, i + j + k))
            k //= 2
        p *= 2
    return pairs


def _bitonic_merge_pairs(n):
    pairs, d = [], n // 2
    while d >= 1:
        pairs += [(i, i + d) for i in range(n) if (i & d) == 0]
        d //= 2
    return pairs


_SORT_PAIRS = _oddeven_merge_sort_pairs(PEER_TOPK)
_MERGE_PAIRS = _bitonic_merge_pairs(PEER_TOPK)


def _group_mean_sq(v, group):
    w = v.shape[-1]
    r = lax.broadcasted_iota(jnp.int32, (w, w), 0) // group
    c = lax.broadcasted_iota(jnp.int32, (w, w), 1) // group
    bd = jnp.where(r == c, 1.0, 0.0).astype(BF16)
    vv = v * v
    hi = vv.astype(BF16)
    lo = (vv - hi.astype(F32)).astype(BF16)
    s = jnp.dot(hi, bd, preferred_element_type=F32) + jnp.dot(lo, bd, preferred_element_type=F32)
    return s * (1.0 / group)


def _mod_kernel(s_ref, w_ref, b_ref, o_ref):
    s = s_ref[...]
    s = s / (1.0 + jnp.exp(-s))
    o_ref[0] = jnp.dot(s, w_ref[0], precision=HI, preferred_element_type=F32) + b_ref[0]


def _mod_all(s_rows, w_mod, b_mod):
    depth, d, n6 = w_mod.shape
    rows = s_rows.shape[0]
    tn = n6 // 4
    return pl.pallas_call(
        _mod_kernel,
        grid=(depth, n6 // tn),
        in_specs=[
            pl.BlockSpec((rows, d), lambda l, j: (0, 0)),
            pl.BlockSpec((1, d, tn), lambda l, j: (l, 0, j)),
            pl.BlockSpec((1, 1, tn), lambda l, j: (l, 0, j)),
        ],
        out_specs=pl.BlockSpec((1, rows, tn), lambda l, j: (l, 0, j)),
        out_shape=jax.ShapeDtypeStruct((depth, rows, n6), F32),
        compiler_params=_cparams("arbitrary", "arbitrary"),
        name="mod",
    )(s_rows, w_mod, b_mod.reshape(depth, 1, n6))


def _norm_mod(x, g, sc, sh):
    ms = jnp.mean(x * x, axis=-1, keepdims=True)
    return (x * lax.rsqrt(ms + RMS_EPS) * g) * (1.0 + sc) + sh


def _nmm_kernel(x_ref, sc_ref, sh_ref, g_ref, w_ref, o_ref):
    h = _norm_mod(x_ref[...], g_ref[...], sc_ref[0], sh_ref[0])
    o_ref[...] = jnp.dot(h.astype(BF16), w_ref[...], preferred_element_type=F32)


def _norm_mod_matmul(x, sc, sh, g, w, row_of, tm):
    n, d = x.shape
    nout = w.shape[1]
    return pl.pallas_call(
        _nmm_kernel,
        grid=(n // tm,),
        in_specs=[
            pl.BlockSpec((tm, d), lambda i: (i, 0)),
            pl.BlockSpec((1, 1, d), lambda i: (row_of(i), 0, 0)),
            pl.BlockSpec((1, 1, d), lambda i: (row_of(i), 0, 0)),
            pl.BlockSpec((1, d), lambda i: (0, 0)),
            pl.BlockSpec((d, nout), lambda i: (0, 0)),
        ],
        out_specs=pl.BlockSpec((tm, nout), lambda i: (i, 0)),
        out_shape=jax.ShapeDtypeStruct((n, nout), F32),
        compiler_params=_cparams("parallel"),
        name="norm_mod_inproj",
    )(x, sc, sh, g, w)


def _shift_rows(z, prev_row, next_row):
    tm = z.shape[0]
    row = lax.broadcasted_iota(jnp.int32, z.shape, 0)
    up = jnp.where(row == 0, prev_row, pltpu.roll(z, 1, 0))
    dn = jnp.where(row == tm - 1, next_row, pltpu.roll(z, tm - 1, 0))
    return up, dn


def _mix_ab_kernel(p_ref, hbp_ref, gcp_ref, hbn_ref, gcn_ref, x_ref, g1_ref, ws_ref, bias_ref, vg_ref, cw_ref,
                   wout_ref, o_ref, *, tiles_per_seq):
    i = pl.program_id(0)
    tm = p_ref.shape[0]
    aw, bw = A_WIDTH, B_WIDTH
    u = _gelu(p_ref[:, 0:aw])
    v = _gelu(p_ref[:, aw:2 * aw])
    vn = (v * lax.rsqrt(_group_mean_sq(v, A_HEAD_DIM) + RMS_EPS) * vg_ref[...]).astype(BF16)
    sv_chunks = []
    for cidx in range(tm // CHUNK):
        vc = vn[cidx * CHUNK:(cidx + 1) * CHUNK]
        heads = [jnp.dot(ws_ref[h], vc[:, h * A_HEAD_DIM:(h + 1) * A_HEAD_DIM], preferred_element_type=F32)
                 for h in range(A_HEADS)]
        sv_chunks.append(jnp.concatenate(heads, axis=-1) + bias_ref[...])
    sv = jnp.concatenate(sv_chunks, axis=0) if len(sv_chunks) > 1 else sv_chunks[0]
    ya = u * sv

    o0 = 2 * aw
    hb = p_ref[:, o0:o0 + bw]
    gb = p_ref[:, o0 + bw:o0 + 2 * bw]
    gc = p_ref[:, o0 + 2 * bw:o0 + 3 * bw]
    z = gc * hb
    pos = i % tiles_per_seq
    zp = jnp.where(pos == 0, 0.0, gcp_ref[SUBLANES - 1:SUBLANES, :] * hbp_ref[SUBLANES - 1:SUBLANES, :])
    zn = jnp.where(pos == tiles_per_seq - 1, 0.0, gcn_ref[0:1, :] * hbn_ref[0:1, :])
    z_up, z_dn = _shift_rows(z, zp, zn)
    yb = gb * (cw_ref[0:1, :] * z_up + cw_ref[1:2, :] * z + cw_ref[2:3, :] * z_dn)

    y = jnp.concatenate([ya, yb], axis=-1).astype(BF16)
    o_ref[...] = x_ref[...] + g1_ref[0] * jnp.dot(y, wout_ref[...], preferred_element_type=F32)


def _mix_ab(p, x, g1, ws, bias_full, vg, cw, w_out, row_of, seq_len, tm):
    n, d = x.shape
    nin = p.shape[1]
    nb8 = n // SUBLANES
    r8 = tm // SUBLANES
    hb_blk = (2 * A_WIDTH) // B_WIDTH
    gc_blk = hb_blk + 2
    prev = lambda i: jnp.maximum(i * r8 - 1, 0)
    nxt = lambda i: jnp.minimum((i + 1) * r8, nb8 - 1)
    full = lambda a: pl.BlockSpec(a.shape, lambda i: (0,) * a.ndim)
    return pl.pallas_call(
        functools.partial(_mix_ab_kernel, tiles_per_seq=seq_len // tm),
        grid=(n // tm,),
        in_specs=[
            pl.BlockSpec((tm, nin), lambda i: (i, 0)),
            pl.BlockSpec((SUBLANES, B_WIDTH), lambda i: (prev(i), hb_blk)),
            pl.BlockSpec((SUBLANES, B_WIDTH), lambda i: (prev(i), gc_blk)),
            pl.BlockSpec((SUBLANES, B_WIDTH), lambda i: (nxt(i), hb_blk)),
            pl.BlockSpec((SUBLANES, B_WIDTH), lambda i: (nxt(i), gc_blk)),
            pl.BlockSpec((tm, d), lambda i: (i, 0)),
            pl.BlockSpec((1, 1, d), lambda i: (row_of(i), 0, 0)),
            full(ws), full(bias_full), full(vg), full(cw), full(w_out),
        ],
        out_specs=pl.BlockSpec((tm, d), lambda i: (i, 0)),
        out_shape=jax.ShapeDtypeStruct((n, d), F32),
        compiler_params=_cparams("parallel"),
        name="mix_ab",
    )(p, p, p, p, p, x, g1, ws, bias_full, vg, cw, w_out)


def _qk_prep_kernel(q_ref, k_ref, v_ref, cos_ref, sin_ref, qg_ref, kg_ref, *rest):
    qm_ref, ko_ref, vo_ref = rest[-3:]
    lane = lax.broadcasted_iota(jnp.int32, q_ref.shape, 1)
    first = (lane % ROPE_AXIS_DIM) < (ROPE_AXIS_DIM // 2)
    w = q_ref.shape[1]
    half = ROPE_AXIS_DIM // 2

    def prep(t, g):
        t = t * lax.rsqrt(_group_mean_sq(t, D_QK) + RMS_EPS) * g
        partner = jnp.where(first, pltpu.roll(t, w - half, 1), pltpu.roll(t, half, 1))
        return t * cos_ref[...] + partner * sin_ref[...]

    q = prep(q_ref[...], qg_ref[...]) * (ATTN_SCALE * math.log2(math.e))
    in_h0 = (lane % (2 * D_QK)) < D_QK
    qm_ref[0] = jnp.where(in_h0, q, 0.0).astype(BF16)
    qm_ref[1] = jnp.where(in_h0, 0.0, q).astype(BF16)
    ko_ref[...] = prep(k_ref[...], kg_ref[...]).astype(BF16)
    vo_ref[...] = v_ref[...].astype(BF16)


def _qk_prep(p, cos_t, sin_t, qg, kg, tab_of, tm, kv_rows, kv_row_of, kv_bufs=None):
    n = p.shape[0]
    q_blk = HY_IN // QK_W
    in_specs = [
        pl.BlockSpec((tm, QK_W), lambda i: (i, q_blk)),
        pl.BlockSpec((tm, QK_W), lambda i: (i, q_blk + 1)),
        pl.BlockSpec((tm, V_W), lambda i: (i, q_blk + 2)),
        pl.BlockSpec((tm, QK_W), lambda i: (tab_of(i), 0)),
        pl.BlockSpec((tm, QK_W), lambda i: (tab_of(i), 0)),
        pl.BlockSpec((1, QK_W), lambda i: (0, 0)),
        pl.BlockSpec((1, QK_W), lambda i: (0, 0)),
    ]
    args = [p, p, p, cos_t, sin_t, qg, kg]
    aliases = {}
    if kv_bufs is not None:
        aliases = {len(args): 1, len(args) + 1: 2}
        in_specs += [pl.BlockSpec(memory_space=pl.ANY)] * 2
        args += list(kv_bufs)
    return pl.pallas_call(
        _qk_prep_kernel,
        grid=(n // tm,),
        in_specs=in_specs,
        out_specs=[
            pl.BlockSpec((2, tm, QK_W), lambda i: (0, i, 0)),
            pl.BlockSpec((tm, QK_W), lambda i: (kv_row_of(i), 0)),
            pl.BlockSpec((tm, V_W), lambda i: (kv_row_of(i), 0)),
        ],
        out_shape=[jax.ShapeDtypeStruct((2, n, QK_W), BF16), jax.ShapeDtypeStruct((kv_rows, QK_W), BF16),
                   jax.ShapeDtypeStruct((kv_rows, V_W), BF16)],
        input_output_aliases=aliases,
        compiler_params=_cparams("parallel"),
        name="qk_prep",
    )(*args)


def _diff_attn_kernel(q0_ref, q1_ref, k_ref, v_ref, lq1_ref, lk1_ref, lq2_ref, lk2_ref, sg_ref, o_ref, *, lam_init):
    lam = (jnp.exp(jnp.sum(lq1_ref[...] * lk1_ref[...], axis=-1, keepdims=True))
           - jnp.exp(jnp.sum(lq2_ref[...] * lk2_ref[...], axis=-1, keepdims=True)) + lam_init)
    k = k_ref[0]
    v = v_ref[0]

    def half(q):
        s = lax.dot_general(q, k, (((1,), (1,)), ((), ())), preferred_element_type=F32)
        m = jnp.max(s, axis=-1, keepdims=True)
        p = jnp.exp2(s - m)
        l = jnp.sum(p, axis=-1, keepdims=True)
        return jnp.dot(p.astype(BF16), v, preferred_element_type=F32) / l

    o = half(q0_ref[0]) - lam * half(q1_ref[0])
    ms = jnp.mean(o * o, axis=-1, keepdims=True)
    o_ref[...] = (o * lax.rsqrt(ms + RMS_EPS) * sg_ref[...]) * (1.0 - lam_init)


def _diff_attn(qm, k_all, v_all, lam_vecs, subln_g, lam_init, batch, lq, tq, lk):
    nq = lq // tq
    vec = lambda a: pl.BlockSpec(a.shape, lambda b, h, i: (0, 0))
    return pl.pallas_call(
        functools.partial(_diff_attn_kernel, lam_init=lam_init),
        grid=(batch, D_HEADS, nq),
        in_specs=[
            pl.BlockSpec((1, tq, D_V), lambda b, h, i: (0, b * nq + i, h)),
            pl.BlockSpec((1, tq, D_V), lambda b, h, i: (1, b * nq + i, h)),
            pl.BlockSpec((1, lk, D_V), lambda b, h, i: (b, 0, h)),
            pl.BlockSpec((1, lk, D_V), lambda b, h, i: (b, 0, h)),
            vec(lam_vecs[0]), vec(lam_vecs[1]), vec(lam_vecs[2]), vec(lam_vecs[3]), vec(subln_g),
        ],
        out_specs=pl.BlockSpec((tq, D_V), lambda b, h, i: (b * nq + i, h)),
        out_shape=jax.ShapeDtypeStruct((batch * lq, V_W), F32),
        compiler_params=_cparams("parallel", "parallel", "arbitrary"),
        name="diff_attn",
    )(qm, qm, k_all, v_all, *lam_vecs, subln_g)


def _hy_pre_kernel(p_ref, pp_ref, pn_ref, sw_ref, sb_ref, x0_ref, u_ref, ub_ref, *, tiles_per_seq):
    i = pl.program_id(0)
    pos = i % tiles_per_seq
    z = p_ref[...]
    zp = jnp.where(pos == 0, 0.0, pp_ref[SUBLANES - 1:SUBLANES, :])
    zn = jnp.where(pos == tiles_per_seq - 1, 0.0, pn_ref[0:1, :])
    z_up, z_dn = _shift_rows(z, zp, zn)
    c = sw_ref[0:1, :] * z_up + sw_ref[1:2, :] * z + sw_ref[2:3, :] * z_dn + sb_ref[...]
    cw = C_WIDTH
    u = c[:, cw:2 * cw] * c[:, 2 * cw:3 * cw]
    x0_ref[...] = c[:, 0:cw]
    u_ref[...] = u
    ub_ref[...] = u.astype(BF16)


def _hy_pre(p, sw, sb, seq_len, tm):
    n = p.shape[0]
    nb8 = n // SUBLANES
    r8 = tm // SUBLANES
    prev = lambda i: jnp.maximum(i * r8 - 1, 0)
    nxt = lambda i: jnp.minimum((i + 1) * r8, nb8 - 1)
    o_spec = pl.BlockSpec((tm, C_WIDTH), lambda i: (i, 0))
    return pl.pallas_call(
        functools.partial(_hy_pre_kernel, tiles_per_seq=seq_len // tm),
        grid=(n // tm,),
        in_specs=[
            pl.BlockSpec((tm, HY_IN), lambda i: (i, 0)),
            pl.BlockSpec((SUBLANES, HY_IN), lambda i: (prev(i), 0)),
            pl.BlockSpec((SUBLANES, HY_IN), lambda i: (nxt(i), 0)),
            pl.BlockSpec((3, HY_IN), lambda i: (0, 0)),
            pl.BlockSpec((1, HY_IN), lambda i: (0, 0)),
        ],
        out_specs=[o_spec, o_spec, o_spec],
        out_shape=[jax.ShapeDtypeStruct((n, C_WIDTH), F32), jax.ShapeDtypeStruct((n, C_WIDTH), F32),
                   jax.ShapeDtypeStruct((n, C_WIDTH), BF16)],
        compiler_params=_cparams("parallel"),
        name="hyena_pre",
    )(p, p, p, sw, sb)


def _hy_filter_kernel(w1_ref, b1_ref, w2_ref, b2_ref, w3_ref, b3_ref, wo_ref, fr_ref, dl_ref, o_ref):
    L = o_ref.shape[0]
    fw = w1_ref.shape[0]
    bands = (HY_EMB - 1) // 2
    row = lax.broadcasted_iota(jnp.int32, (L, fw), 0).astype(F32)
    col = lax.broadcasted_iota(jnp.int32, (L, fw), 1)
    t = row / (L - 1.0)
    ang = (2.0 * math.pi) * row / L
    step = (bands - 1 - 1e-4) / (bands - 1)
    band = jnp.where(col <= bands, col - 1, col - 1 - bands).astype(F32)
    arg = (1e-4 + band * step) * ang
    feats = jnp.where(col == 0, t, jnp.where(col <= bands, jnp.cos(arg), jnp.where(col < HY_EMB, -jnp.sin(arg), 0.0)))
    fr = fr_ref[...]
    mm = lambda a, w: jnp.dot(a, w, precision=HI, preferred_element_type=F32)
    a = jnp.sin(fr * (mm(feats, w1_ref[...]) + b1_ref[...]))
    a = jnp.sin(fr * (mm(a, w2_ref[...]) + b2_ref[...]))
    a = jnp.sin(fr * (mm(a, w3_ref[...]) + b3_ref[...]))
    hf = mm(a, wo_ref[...])
    rowc = lax.broadcasted_iota(jnp.int32, (L, C_WIDTH), 0)
    tc = rowc.astype(F32) / (L - 1.0)
    decay = jnp.exp(-tc * dl_ref[...])
    h_fwd = hf[:, :C_WIDTH] * decay
    h_bwd = jnp.where(rowc == 0, 0.0, hf[:, C_WIDTH:] * decay)
    norm = jnp.sum(jnp.abs(h_fwd) + jnp.abs(h_bwd), axis=0, keepdims=True)
    o_ref[:, :C_WIDTH] = (h_fwd / norm).astype(BF16)
    o_ref[:, C_WIDTH:] = (h_bwd / norm).astype(BF16)


def _hy_filter(L, fp, deltas):
    full = lambda a: pl.BlockSpec(a.shape, lambda: (0,) * a.ndim)
    args = (*fp, deltas)
    return pl.pallas_call(
        _hy_filter_kernel,
        in_specs=[full(a) for a in args],
        out_specs=pl.BlockSpec((L, 2 * C_WIDTH), lambda: (0, 0)),
        out_shape=jax.ShapeDtypeStruct((L, 2 * C_WIDTH), BF16),
        compiler_params=pltpu.CompilerParams(vmem_limit_bytes=VMEM_LIMIT),
        name="hyena_filter",
    )(*args)


def _mm_kernel(a_ref, b_ref, o_ref):
    o_ref[...] = jnp.dot(a_ref[...], b_ref[...], preferred_element_type=F32)


def _matmul(a, b, tm):
    m, k = a.shape
    n = b.shape[1]
    return pl.pallas_call(
        _mm_kernel,
        grid=(m // tm,),
        in_specs=[pl.BlockSpec((tm, k), lambda i: (i, 0)), pl.BlockSpec((k, n), lambda i: (0, 0))],
        out_specs=pl.BlockSpec((tm, n), lambda i: (i, 0)),
        out_shape=jax.ShapeDtypeStruct((m, n), F32),
        compiler_params=_cparams("parallel"),
        name="dft_filter",
    )(a, b)


def _hy_conv_kernel(u_ref, wt_ref, wb_ref, w2t_ref, w2b_ref, at_ref, ab_ref, y_ref):
    j = pl.program_id(1)
    cw = C_WIDTH
    u = u_ref[...]
    uc = jnp.dot(wt_ref[...], u, preferred_element_type=F32)
    us = jnp.dot(wb_ref[...], u, preferred_element_type=F32)
    kre = at_ref[:, :cw] + at_ref[:, cw:]
    kim = ab_ref[:, cw:] - ab_ref[:, :cw]
    knyq = ab_ref[:, :cw] + ab_ref[:, cw:]
    row = lax.broadcasted_iota(jnp.int32, kre.shape, 0)
    special = jnp.logical_and(row == 0, j == 0)
    p2 = jnp.where(special, 0.0, kim)
    p4 = jnp.where(special, knyq, -kre)
    yt = (uc * kre + us * p2).astype(BF16)
    yb = (uc * p2 + us * p4).astype(BF16)
    L = y_ref.shape[0]
    tr = min(L, 512)

    def inverse_rows(rows, first):
        contrib = (jnp.dot(w2t_ref[rows, :], yt, preferred_element_type=F32)
                   + jnp.dot(w2b_ref[rows, :], yb, preferred_element_type=F32))
        y_ref[rows, :] = contrib if first else y_ref[rows, :] + contrib

    @pl.when(j == 0)
    def _():
        for c in range(L // tr):
            inverse_rows(slice(c * tr, (c + 1) * tr), True)

    @pl.when(j > 0)
    def _():
        for c in range(L // tr):
            inverse_rows(slice(c * tr, (c + 1) * tr), False)


def _hy_conv(ub, w, w2, a_spec, batch, L, rb):
    ns = L // rb
    return pl.pallas_call(
        _hy_conv_kernel,
        grid=(batch, ns),
        in_specs=[
            pl.BlockSpec((L, C_WIDTH), lambda b, j: (b, 0)),
            pl.BlockSpec((rb, L), lambda b, j: (j, 0)),
            pl.BlockSpec((rb, L), lambda b, j: (j + ns, 0)),
            pl.BlockSpec((L, rb), lambda b, j: (0, j)),
            pl.BlockSpec((L, rb), lambda b, j: (0, j + ns)),
            pl.BlockSpec((rb, 2 * C_WIDTH), lambda b, j: (j, 0)),
            pl.BlockSpec((rb, 2 * C_WIDTH), lambda b, j: (j + ns, 0)),
        ],
        out_specs=pl.BlockSpec((L, C_WIDTH), lambda b, j: (b, 0)),
        out_shape=jax.ShapeDtypeStruct((batch * L, C_WIDTH), F32),
        compiler_params=_cparams("parallel", "arbitrary"),
        name="hyena_conv",
    )(ub, w, w, w2, w2, a_spec, a_spec)


def _proj_cd_kernel(x0_ref, y_ref, u_ref, skip_ref, yd_ref, w_ref, x_ref, g1_ref, o_ref):
    yh = (x0_ref[...] * (y_ref[...] + u_ref[...] * skip_ref[...])).astype(BF16)
    acc = jnp.dot(yh, w_ref[:C_WIDTH, :], preferred_element_type=F32)
    acc += jnp.dot(yd_ref[...].astype(BF16), w_ref[C_WIDTH:, :], preferred_element_type=F32)
    o_ref[...] = x_ref[...] + g1_ref[0] * acc


def _proj_cd(x0, y, u, skip, yd, w_out, x, g1, row_of, tm):
    n, d = x.shape
    half = pl.BlockSpec((tm, C_WIDTH), lambda i: (i, 0))
    return pl.pallas_call(
        _proj_cd_kernel,
        grid=(n // tm,),
        in_specs=[
            half, half, half,
            pl.BlockSpec((1, C_WIDTH), lambda i: (0, 0)),
            half,
            pl.BlockSpec(w_out.shape, lambda i: (0, 0)),
            pl.BlockSpec((tm, d), lambda i: (i, 0)),
            pl.BlockSpec((1, 1, d), lambda i: (row_of(i), 0, 0)),
        ],
        out_specs=pl.BlockSpec((tm, d), lambda i: (i, 0)),
        out_shape=jax.ShapeDtypeStruct((n, d), F32),
        compiler_params=_cparams("parallel"),
        name="proj_cd",
    )(x0, y, u, skip, yd, w_out, x, g1)


def _dft_mats(L):
    m = 2 * L
    r = jnp.arange(L, dtype=jnp.int32)
    blk = math.gcd(L, 64)
    phase = lambda cols: ((r[:, None] * cols[None, :]) % m).astype(F32) * (2.0 * math.pi / m)
    a_hi = phase(jnp.arange(L // blk, dtype=jnp.int32) * blk)[:, :, None]
    a_lo = phase(jnp.arange(blk, dtype=jnp.int32))[:, None, :]
    c = (jnp.cos(a_hi) * jnp.cos(a_lo) - jnp.sin(a_hi) * jnp.sin(a_lo)).reshape(L, L)
    s = (jnp.sin(a_hi) * jnp.cos(a_lo) + jnp.cos(a_hi) * jnp.sin(a_lo)).reshape(L, L)
    alt = jnp.where(r % 2 == 0, 1.0, -1.0).astype(F32)
    is0 = (r == 0)[:, None]
    fwd = jnp.concatenate([c, jnp.where(is0, alt[None, :], s)], axis=0)
    inv_top = jnp.where(is0, 1.0 / m, c * (2.0 / m))
    inv_bot = jnp.where(is0, alt[None, :] / m, s * (-2.0 / m))
    inv = jnp.concatenate([inv_top, inv_bot], axis=0).T
    return fwd.astype(BF16), inv.astype(BF16)


def _peer_kernel(x_ref, sc_ref, sh_ref, g2_ref, ng_ref, wqT_ref, keys_ref, u0_ref, u_ref, vT_ref, o_ref,
                 h2T_s, sT_s, tops_s, th_s, iz_s, aT_s, bT_s, thT_s, act0_s, act1_s, w0_s, w1_s,
                 cutb0_s, cutb1_s, ab0_s, ab1_s, acc_s, *, nblk):
    e = pl.program_id(1)
    nk = keys_ref.shape[1]
    T = x_ref.shape[0]
    eb = u_ref.shape[0]
    nh = PEER_HEADS
    ntc = T // LANES
    neg = -jnp.inf

    def fill_rows(cutb, ab, blk):
        base = blk * SUBLANES if isinstance(blk, int) else pl.multiple_of(blk * SUBLANES, SUBLANES)
        for h in range(nh):
            cut8 = thT_s[h, pl.ds(base, SUBLANES), :]
            a8 = aT_s[h, pl.ds(base, SUBLANES), :]
            for ii in range(SUBLANES):
                cutb[h, ii] = jnp.broadcast_to(cut8[ii:ii + 1, :], (SUBLANES, T))
                ab[h, ii] = jnp.broadcast_to(a8[ii:ii + 1, :], (SUBLANES, T))

    @pl.when(e == 0)
    def _route():
        h2 = _norm_mod(x_ref[...], ng_ref[...], sc_ref[0], sh_ref[0])
        h2T = h2.T.astype(BF16)
        h2T_s[...] = h2T
        qT = jnp.dot(wqT_ref[...], h2T, preferred_element_type=F32).astype(BF16)
        pk = keys_ref.shape[2]
        for hp in range(2 * nh):
            sT_s[hp] = jnp.dot(keys_ref[hp], qT[hp * pk:(hp + 1) * pk, :], preferred_element_type=F32)

        def exchange(v, pairs):
            for i, j in pairs:
                v[i], v[j] = jnp.maximum(v[i], v[j]), jnp.minimum(v[i], v[j])

        def top_body(hp, carry):
            for tc in range(ntc):
                sl = slice(tc * LANES, (tc + 1) * LANES)
                v = [sT_s[hp, i * SUBLANES:(i + 1) * SUBLANES, sl] for i in range(PEER_TOPK)]
                exchange(v, _SORT_PAIRS)
                shift = SUBLANES // 2
                while shift >= 1:
                    w = [pltpu.roll(a, shift, 0) for a in v]
                    v = [jnp.maximum(v[i], w[PEER_TOPK - 1 - i]) for i in range(PEER_TOPK)]
                    exchange(v, _MERGE_PAIRS)
                    shift //= 2
                for i in range(PEER_TOPK):
                    tops_s[hp, i:i + 1, sl] = v[i][0:1]
            return carry

        lax.fori_loop(0, 2 * nh, top_body, 0)

        act0_s[...] = jnp.dot(u0_ref[...], h2T, preferred_element_type=F32)

        pairs = [(i, j) for i in range(PEER_TOPK) for j in range(PEER_TOPK // (i + 1))]
        for tc in range(ntc):
            sl = slice(tc * LANES, (tc + 1) * LANES)
            A = [jnp.concatenate([tops_s[2 * h, i:i + 1, sl] for h in range(nh)], axis=0) for i in range(PEER_TOPK)]
            B = [jnp.concatenate([tops_s[2 * h + 1, j:j + 1, sl] for h in range(nh)], axis=0)
                 for j in range(PEER_TOPK)]
            cand = [A[i] + B[j] for (i, j) in pairs]
            work = list(cand)
            tau = None
            for r in range(PEER_TOPK):
                tau = functools.reduce(jnp.maximum, work)
                if r < PEER_TOPK - 1:
                    work = [jnp.where(c >= tau, neg, c) for c in work]
            m0 = cand[0]
            z = functools.reduce(lambda a, b: a + b, [jnp.where(c >= tau, jnp.exp(c - m0), 0.0) for c in cand])
            iz = 1.0 / z
            for h in range(nh):
                iz_s[h, 0:1, sl] = iz[h:h + 1]
            for i in range(PEER_TOPK):
                row = [jnp.where(cand[n] >= tau, B[j], jnp.inf) for n, (ii, j) in enumerate(pairs) if ii == i]
                cut = functools.reduce(jnp.minimum, row)
                for h in range(nh):
                    th_s[h, i:i + 1, sl] = cut[h:h + 1]

        def head_body(h, carry):
            s0 = sT_s[2 * h]
            s1 = sT_s[2 * h + 1]
            a0 = tops_s[2 * h, 0:1, :]
            b0 = tops_s[2 * h + 1, 0:1, :]
            aT_s[h] = jnp.exp(s0 - a0) * (0.5 * iz_s[h, 0:1, :])
            bT_s[h] = jnp.exp(s1 - b0)
            th = jnp.full(s0.shape, jnp.inf, F32)
            for i in range(PEER_TOPK):
                th = jnp.where(s0 == tops_s[2 * h, i:i + 1, :], th_s[h, i:i + 1, :], th)
            thT_s[h] = th
            return carry

        lax.fori_loop(0, nh, head_body, 0)

        w1_s[...] = jnp.zeros(w1_s.shape, BF16)
        acc_s[...] = jnp.zeros(acc_s.shape, F32)
        fill_rows(cutb0_s, ab0_s, 0)

    def stage(act_w, act_r, w_w, w_r, cutb_w, ab_w, cutb_r, ab_r):
        nrc = nk // GATE_ROWS
        nvr = GATE_ROWS // SUBLANES
        mrows = eb // ntc
        drows = acc_s.shape[0] // ntc
        def mxu_slice(tc):
            ru = slice(tc * mrows, (tc + 1) * mrows)
            act_w[ru, :] = jnp.dot(u_ref[ru, :], h2T_s[...], preferred_element_type=F32)
            rd = slice(tc * drows, (tc + 1) * drows)
            acc_s[rd, :] += jnp.dot(vT_ref[rd, :], w_r[...], preferred_element_type=F32)

        for tc in range(ntc):
            sl = slice(tc * LANES, (tc + 1) * LANES)
            for rc in range(nrc):
                g = [[jnp.zeros((SUBLANES, LANES), F32) for _ in range(nvr)] for _ in range(SUBLANES)]
                for h in range(nh):
                    s1c = [sT_s[2 * h + 1, rc * GATE_ROWS + v * SUBLANES:rc * GATE_ROWS + (v + 1) * SUBLANES, sl]
                           for v in range(nvr)]
                    bc = [bT_s[h, rc * GATE_ROWS + v * SUBLANES:rc * GATE_ROWS + (v + 1) * SUBLANES, sl]
                          for v in range(nvr)]
                    for ii in range(SUBLANES):
                        cut = cutb_r[h, ii, :, sl]
                        a = ab_r[h, ii, :, sl]
                        for v in range(nvr):
                            g[ii][v] = g[ii][v] + jnp.where(s1c[v] >= cut, a * bc[v], 0.0)
                for ii in range(SUBLANES):
                    rows = slice(ii * nk + rc * GATE_ROWS, ii * nk + (rc + 1) * GATE_ROWS)
                    gi = jnp.concatenate(g[ii], axis=0)
                    act = act_r[rows, sl]
                    w_w[rows, sl] = (gi * (act * (1.0 + lax.erf(act * (2.0 ** -0.5))))).astype(BF16)
            mxu_slice(tc)
        fill_rows(cutb_w, ab_w, jnp.minimum(e + 1, nblk - 1))

    @pl.when(jnp.logical_and(e < nblk, e % 2 == 0))
    def _():
        stage(act1_s, act0_s, w0_s, w1_s, cutb1_s, ab1_s, cutb0_s, ab0_s)

    @pl.when(jnp.logical_and(e < nblk, e % 2 == 1))
    def _():
        stage(act0_s, act1_s, w1_s, w0_s, cutb0_s, ab0_s, cutb1_s, ab1_s)

    @pl.when(e == nblk)
    def _():
        w_last = w0_s if (nblk - 1) % 2 == 0 else w1_s
        acc = acc_s[...] + jnp.dot(vT_ref[...], w_last[...], preferred_element_type=F32)
        o_ref[...] = x_ref[...] + g2_ref[0] * acc.T


def _peer(x, sc, sh, g2, ng, wqT, keys, u_b, vT_b, row_of, T, eb):
    n, d = x.shape
    n_exp = u_b.shape[0]
    nk = keys.shape[1]
    nh = PEER_HEADS
    modspec = pl.BlockSpec((1, 1, d), lambda i, e: (row_of(i), 0, 0))
    nblk = n_exp // eb
    assert eb == nk * SUBLANES and nk % GATE_ROWS == 0 and T % LANES == 0 and nk == SUBLANES * PEER_TOPK
    once = pl.Buffered(1)
    return pl.pallas_call(
        functools.partial(_peer_kernel, nblk=nblk),
        grid=(n // T, nblk + 1),
        in_specs=[
            pl.BlockSpec((T, d), lambda i, e: (i, 0)),
            modspec, modspec, modspec,
            pl.BlockSpec((1, d), lambda i, e: (0, 0)),
            pl.BlockSpec(wqT.shape, lambda i, e: (0, 0), pipeline_mode=once),
            pl.BlockSpec(keys.shape, lambda i, e: (0, 0, 0), pipeline_mode=once),
            pl.BlockSpec((eb, d), lambda i, e: (0, 0), pipeline_mode=once),
            pl.BlockSpec((eb, d), lambda i, e: (jnp.minimum(e + 1, nblk - 1), 0)),
            pl.BlockSpec((d, eb), lambda i, e: (0, jnp.clip(e - 1, 0, nblk - 1))),
        ],
        out_specs=pl.BlockSpec((T, d), lambda i, e: (i, 0)),
        out_shape=jax.ShapeDtypeStruct((n, d), F32),
        scratch_shapes=[
            pltpu.VMEM((d, T), BF16),
            pltpu.VMEM((2 * nh, nk, T), F32),
            pltpu.VMEM((2 * nh, PEER_TOPK, T), F32),
            pltpu.VMEM((nh, PEER_TOPK, T), F32),
            pltpu.VMEM((nh, SUBLANES, T), F32),
            pltpu.VMEM((nh, nk, T), F32),
            pltpu.VMEM((nh, nk, T), F32),
            pltpu.VMEM((nh, nk, T), F32),
            pltpu.VMEM((eb, T), F32),
            pltpu.VMEM((eb, T), F32),
            pltpu.VMEM((eb, T), BF16),
            pltpu.VMEM((eb, T), BF16),
            pltpu.VMEM((nh, SUBLANES, SUBLANES, T), F32),
            pltpu.VMEM((nh, SUBLANES, SUBLANES, T), F32),
            pltpu.VMEM((nh, SUBLANES, SUBLANES, T), F32),
            pltpu.VMEM((nh, SUBLANES, SUBLANES, T), F32),
            pltpu.VMEM((d, T), F32),
        ],
        compiler_params=_cparams("parallel", "arbitrary"),
        name="peer",
    )(x, sc, sh, g2, ng, wqT, keys, u_b, u_b, vT_b)


def _pick(n, pref):
    t = pref
    while n % t:
        t //= 2
    return t


def kernel(x, c, ctx, c_ctx, w_mod, b_mod, norm1_g, norm2_g, ab_w_in, a_ws, a_bs, a_vnorm_g, b_conv_w, ab_w_out,
           cd_w_in, c_short_w, c_short_b, c_filt_w1, c_filt_b1, c_filt_w2, c_filt_b2, c_filt_w3, c_filt_b3,
           c_filt_wout, c_filt_freq, c_skip, d_q_norm_g, d_k_norm_g, d_lambda_q1, d_lambda_k1, d_lambda_q2,
           d_lambda_k2, d_subln_g, cd_w_out, peer_wq, peer_keys, peer_u, peer_v):
    B, L, D = x.shape
    Lc = ctx.shape[1]
    depth = w_mod.shape[0]
    n_lat, n_ctx = B * L, B * Lc

    tm_l, tm_c = _pick(L, 512), _pick(Lc, 256)
    tm_kv = _pick(math.gcd(L, Lc), 256)
    tq_l, tq_c = _pick(L, 256), _pick(Lc, 256)
    tp_l, tp_c = _pick(L, 512), _pick(n_ctx, 512)
    eb = 1024

    mod_rows = 2 * SUBLANES
    s_rows = jnp.zeros((mod_rows, D), F32).at[:B].set(c).at[B].set(c_ctx)
    mod = _mod_all(s_rows, w_mod, b_mod).reshape(depth, mod_rows, N_MOD, 1, D)

    xl = x.reshape(n_lat, D)
    xc = ctx.reshape(n_ctx, D)

    def lat_row(t):
        return lambda i: (i * t) // L

    ctx_row = lambda i: B

    col = jnp.arange(QK_W)
    inv = ROPE_BASE ** (-jnp.arange(0, ROPE_AXIS_DIM, 2, dtype=F32) / ROPE_AXIS_DIM)
    pos = jnp.arange(L)
    rows_f = (pos // GRID_W).astype(F32)
    cols_f = (pos % GRID_W).astype(F32)
    is_row_part = (col % D_QK) < ROPE_AXIS_DIM
    ang = jnp.where(is_row_part[None, :], rows_f[:, None], cols_f[:, None]) * inv[col % (ROPE_AXIS_DIM // 2)][None, :]
    cos_l = jnp.cos(ang)
    sin_l = jnp.where(((col % ROPE_AXIS_DIM) < ROPE_AXIS_DIM // 2)[None, :], -jnp.sin(ang), jnp.sin(ang))
    cos_c = jnp.ones((tm_kv, QK_W), F32)
    sin_c = jnp.zeros((tm_kv, QK_W), F32)

    deltas = jnp.abs(jnp.linspace(math.log(HY_TARGET) / HY_DECAY_LONG_PCT, math.log(HY_TARGET) / HY_DECAY_SHORT_PCT,
                                  C_WIDTH, dtype=F32))[None, :]

    for l in range(depth):
        last = l == depth - 1
        odd = l % 2 == 1
        i = l // 2
        m = lambda k: mod[l, :, k]
        sh1, sc1, g1, sh2, sc2, g2 = (m(k) for k in range(N_MOD))
        n1 = norm1_g[l][None, :]
        need_ctx = (not last) or odd

        if not odd:
            w_in = ab_w_in[i].astype(BF16)
            w_out = ab_w_out[i].astype(BF16)
            ws = a_ws[i].astype(BF16)
            bias_full = jnp.repeat(a_bs[i].T, A_HEAD_DIM, axis=1)
            vg = a_vnorm_g[i][None, :]
            cw = b_conv_w[i]
            p = _norm_mod_matmul(xl, sc1, sh1, n1, w_in, lat_row(tm_l), tm_l)
            xl = _mix_ab(p, xl, g1, ws, bias_full, vg, cw, w_out, lat_row(tm_l), L, tm_l)
            if not last:
                pc = _norm_mod_matmul(xc, sc1, sh1, n1, w_in, ctx_row, tm_c)
                xc = _mix_ab(pc, xc, g1, ws, bias_full, vg, cw, w_out, ctx_row, Lc, tm_c)
        else:
            lam_init = 0.8 - 0.6 * math.exp(-0.3 * l)
            w_in = cd_w_in[i].astype(BF16)
            w_out = cd_w_out[i].astype(BF16)
            qg = jnp.tile(d_q_norm_g[i], QK_W // D_QK)[None, :]
            kg = jnp.tile(d_k_norm_g[i], QK_W // D_QK)[None, :]
            lam_vecs = tuple(v[i][None, :] for v in (d_lambda_q1, d_lambda_k1, d_lambda_q2, d_lambda_k2))
            sg = d_subln_g[i][None, :]
            fw = LANES
            padc = lambda a: jnp.pad(a, ((0, 0), (0, fw - a.shape[1])))
            padr = lambda a: jnp.pad(a, ((0, fw - a.shape[0]), (0, 0)))
            fp = (padc(padr(c_filt_w1[i])), padc(c_filt_b1[i][None, :]), padc(padr(c_filt_w2[i])),
                  padc(c_filt_b2[i][None, :]), padc(padr(c_filt_w3[i])), padc(c_filt_b3[i][None, :]),
                  padr(c_filt_wout[i]), padc(c_filt_freq[i][None, :]))
            sw, sb, skip = c_short_w[i], c_short_b[i][None, :], c_skip[i][None, :]

            def hyena(pp, batch, seq, tm):
                x0, u, ub = _hy_pre(pp, sw, sb, seq, tm)
                fwd, invm = _dft_mats(seq)
                filt = _hy_filter(seq, fp, deltas)
                spec = _matmul(fwd, filt, _pick(2 * seq, 512))
                y = _hy_conv(ub, fwd, invm, spec, batch, seq, _pick(seq, 256))
                return x0, y, u

            pc = _norm_mod_matmul(xc, sc1, sh1, n1, w_in, ctx_row, tm_c)
            kv_blocks = (Lc + L) // tm_kv
            ctx_kv_row = lambda t: (t // (Lc // tm_kv)) * kv_blocks + t % (Lc // tm_kv)
            lat_kv_row = lambda t: (t // (L // tm_kv)) * kv_blocks + Lc // tm_kv + t % (L // tm_kv)
            qmc, k_all, v_all = _qk_prep(pc, cos_c, sin_c, qg, kg, lambda t: 0, tm_kv, B * (Lc + L), ctx_kv_row)
            p = _norm_mod_matmul(xl, sc1, sh1, n1, w_in, lat_row(tm_l), tm_l)
            qm, k_all, v_all = _qk_prep(p, cos_l, sin_l, qg, kg, lambda t: t % (L // tm_kv), tm_kv, B * (Lc + L),
                                        lat_kv_row, (k_all, v_all))
            k_all = k_all.reshape(B, Lc + L, QK_W)
            v_all = v_all.reshape(B, Lc + L, V_W)
            yd = _diff_attn(qm, k_all, v_all, lam_vecs, sg, lam_init, B, L, tq_l, Lc + L)
            x0, y, u = hyena(p, B, L, tm_l)
            xl = _proj_cd(x0, y, u, skip, yd, w_out, xl, g1, lat_row(tm_l), tm_l)
            if not last:
                ydc = _diff_attn(qmc, k_all, v_all, lam_vecs, sg, lam_init, B, Lc, tq_c, Lc)
                x0c, yc, uc = hyena(pc, B, Lc, tm_c)
                xc = _proj_cd(x0c, yc, uc, skip, ydc, w_out, xc, g1, ctx_row, tm_c)

        n2 = norm2_g[l][None, :]
        wqT = peer_wq[l].T.astype(BF16)
        keys = peer_keys[l].reshape(2 * PEER_HEADS, peer_keys.shape[3], peer_keys.shape[4]).astype(BF16)
        u_b = peer_u[l].astype(BF16)
        vT_b = peer_v[l].T.astype(BF16)
        xl = _peer(xl, sc2, sh2, g2, n2, wqT, keys, u_b, vT_b, lat_row(tp_l), tp_l, eb)
        if not last:
            xc = _peer(xc, sc2, sh2, g2, n2, wqT, keys, u_b, vT_b, ctx_row, tp_c, eb)

    return xl.reshape(B, L, D)
```

```python
import functools
import math

import jax
import jax.numpy as jnp
from jax import lax
from jax.experimental import pallas as pl
from jax.experimental.pallas import tpu as pltpu

F32 = jnp.float32
BF16 = jnp.bfloat16
HI = lax.Precision.HIGHEST

GRID_W = 64
N_MOD = 6
RMS_EPS = 1e-6
CHUNK = 128
A_HEADS = 8
A_HEAD_DIM = 64
A_WIDTH = A_HEADS * A_HEAD_DIM
B_WIDTH = 512
C_WIDTH = 512
HY_IN = 3 * C_WIDTH
HY_EMB = 33
HY_DECAY_LONG_PCT = 1.5
HY_DECAY_SHORT_PCT = 0.3
HY_TARGET = 1e-2
D_HEADS = 4
D_QK = 64
D_V = 2 * D_QK
QK_W = D_HEADS * 2 * D_QK
V_W = D_HEADS * D_V
ATTN_SCALE = D_QK ** -0.5
ROPE_AXIS_DIM = D_QK // 2
ROPE_BASE = 10000.0
PEER_HEADS = 8
PEER_TOPK = 16

LANES = 128
SUBLANES = 8
VMEM_LIMIT = 56 * 1024 * 1024
GATE_ROWS = 32
HALO_ROWS = 2 * SUBLANES


def _cparams(*sem):
    return pltpu.CompilerParams(dimension_semantics=sem, vmem_limit_bytes=VMEM_LIMIT)


def _gelu(x):
    return 0.5 * x * (1.0 + lax.erf(x * (2.0 ** -0.5)))


def _oddeven_merge_sort_pairs(n):
    pairs, p = [], 1
    while p < n:
        k = p
        while k >= 1:
            for j in range(k % p, n - k, 2 * k):
                for i in range(min(k, n - j - k)):
                    if (i + j) // (2 * p) == (i + j + k) // (2 * p):
                        pairs.append((i + j, i + j + k))
            k //= 2
        p *= 2
    return pairs


def _bitonic_merge_pairs(n):
    pairs, d = [], n // 2
    while d >= 1:
        pairs += [(i, i + d) for i in range(n) if (i & d) == 0]
        d //= 2
    return pairs


_SORT_PAIRS = _oddeven_merge_sort_pairs(PEER_TOPK)
_MERGE_PAIRS = _bitonic_merge_pairs(PEER_TOPK)


def _group_mean_sq(v, group):
    w = v.shape[-1]
    r = lax.broadcasted_iota(jnp.int32, (w, w), 0) // group
    c = lax.broadcasted_iota(jnp.int32, (w, w), 1) // group
    bd = jnp.where(r == c, 1.0, 0.0).astype(BF16)
    vv = v * v
    hi = vv.astype(BF16)
    lo = (vv - hi.astype(F32)).astype(BF16)
    s = jnp.dot(hi, bd, preferred_element_type=F32) + jnp.dot(lo, bd, preferred_element_type=F32)
    return s * (1.0 / group)


def _mod_kernel(s_ref, w_ref, b_ref, o_ref):
    s = s_ref[...]
    s = s / (1.0 + jnp.exp(-s))
    o_ref[0] = jnp.dot(s, w_ref[0], precision=HI, preferred_element_type=F32) + b_ref[0]


def _mod_all(s_rows, w_mod, b_mod):
    depth, d, n6 = w_mod.shape
    rows = s_rows.shape[0]
    tn = n6 // 4
    return pl.pallas_call(
        _mod_kernel,
        grid=(depth, n6 // tn),
        in_specs=[
            pl.BlockSpec((rows, d), lambda l, j: (0, 0)),
            pl.BlockSpec((1, d, tn), lambda l, j: (l, 0, j)),
            pl.BlockSpec((1, 1, tn), lambda l, j: (l, 0, j)),
        ],
        out_specs=pl.BlockSpec((1, rows, tn), lambda l, j: (l, 0, j)),
        out_shape=jax.ShapeDtypeStruct((depth, rows, n6), F32),
        compiler_params=_cparams("arbitrary", "arbitrary"),
        name="mod",
    )(s_rows, w_mod, b_mod.reshape(depth, 1, n6))


def _norm_mod(x, g, sc, sh):
    ms = jnp.mean(x * x, axis=-1, keepdims=True)
    return (x * lax.rsqrt(ms + RMS_EPS) * g) * (1.0 + sc) + sh


def _nmm_kernel(x_ref, sc_ref, sh_ref, g_ref, w_ref, o_ref):
    h = _norm_mod(x_ref[...], g_ref[...], sc_ref[0], sh_ref[0])
    o_ref[...] = jnp.dot(h.astype(BF16), w_ref[...], preferred_element_type=F32).astype(o_ref.dtype)


def _norm_mod_matmul(x, sc, sh, g, w, row_of, tm):
    n, d = x.shape
    nout = w.shape[1]
    return pl.pallas_call(
        _nmm_kernel,
        grid=(n // tm,),
        in_specs=[
            pl.BlockSpec((tm, d), lambda i: (i, 0)),
            pl.BlockSpec((1, 1, d), lambda i: (row_of(i), 0, 0)),
            pl.BlockSpec((1, 1, d), lambda i: (row_of(i), 0, 0)),
            pl.BlockSpec((1, d), lambda i: (0, 0)),
            pl.BlockSpec((d, nout), lambda i: (0, 0)),
        ],
        out_specs=pl.BlockSpec((tm, nout), lambda i: (i, 0)),
        out_shape=jax.ShapeDtypeStruct((n, nout), BF16),
        compiler_params=_cparams("parallel"),
        name="norm_mod_inproj",
    )(x, sc, sh, g, w)


def _shift_rows(z, prev_row, next_row):
    tm = z.shape[0]
    row = lax.broadcasted_iota(jnp.int32, z.shape, 0)
    up = jnp.where(row == 0, prev_row, pltpu.roll(z, 1, 0))
    dn = jnp.where(row == tm - 1, next_row, pltpu.roll(z, tm - 1, 0))
    return up, dn


def _mix_ab_kernel(p_ref, hbp_ref, gcp_ref, hbn_ref, gcn_ref, x_ref, g1_ref, ws_ref, bias_ref, vg_ref, cw_ref,
                   wout_ref, o_ref, *, tiles_per_seq):
    i = pl.program_id(0)
    tm = p_ref.shape[0]
    aw, bw = A_WIDTH, B_WIDTH
    u = _gelu(p_ref[:, 0:aw].astype(F32))
    v = _gelu(p_ref[:, aw:2 * aw].astype(F32))
    vn = (v * lax.rsqrt(_group_mean_sq(v, A_HEAD_DIM) + RMS_EPS) * vg_ref[...]).astype(BF16)
    sv_chunks = []
    for cidx in range(tm // CHUNK):
        vc = vn[cidx * CHUNK:(cidx + 1) * CHUNK]
        heads = [jnp.dot(ws_ref[h], vc[:, h * A_HEAD_DIM:(h + 1) * A_HEAD_DIM], preferred_element_type=F32)
                 for h in range(A_HEADS)]
        sv_chunks.append(jnp.concatenate(heads, axis=-1) + bias_ref[...])
    sv = jnp.concatenate(sv_chunks, axis=0) if len(sv_chunks) > 1 else sv_chunks[0]
    ya = u * sv

    o0 = 2 * aw
    hb = p_ref[:, o0:o0 + bw].astype(F32)
    gb = p_ref[:, o0 + bw:o0 + 2 * bw].astype(F32)
    gc = p_ref[:, o0 + 2 * bw:o0 + 3 * bw].astype(F32)
    z = gc * hb
    pos = i % tiles_per_seq
    last_row = lambda r: r[...].astype(F32)[HALO_ROWS - 1:HALO_ROWS, :]
    first_row = lambda r: r[...].astype(F32)[0:1, :]
    zp = jnp.where(pos == 0, 0.0, last_row(gcp_ref) * last_row(hbp_ref))
    zn = jnp.where(pos == tiles_per_seq - 1, 0.0, first_row(gcn_ref) * first_row(hbn_ref))
    z_up, z_dn = _shift_rows(z, zp, zn)
    yb = gb * (cw_ref[0:1, :] * z_up + cw_ref[1:2, :] * z + cw_ref[2:3, :] * z_dn)

    y = jnp.concatenate([ya, yb], axis=-1).astype(BF16)
    o_ref[...] = x_ref[...] + g1_ref[0] * jnp.dot(y, wout_ref[...], preferred_element_type=F32)


def _mix_ab(p, x, g1, ws, bias_full, vg, cw, w_out, row_of, seq_len, tm):
    n, d = x.shape
    nin = p.shape[1]
    nb8 = n // HALO_ROWS
    r8 = tm // HALO_ROWS
    hb_blk = (2 * A_WIDTH) // B_WIDTH
    gc_blk = hb_blk + 2
    prev = lambda i: jnp.maximum(i * r8 - 1, 0)
    nxt = lambda i: jnp.minimum((i + 1) * r8, nb8 - 1)
    full = lambda a: pl.BlockSpec(a.shape, lambda i: (0,) * a.ndim)
    return pl.pallas_call(
        functools.partial(_mix_ab_kernel, tiles_per_seq=seq_len // tm),
        grid=(n // tm,),
        in_specs=[
            pl.BlockSpec((tm, nin), lambda i: (i, 0)),
            pl.BlockSpec((HALO_ROWS, B_WIDTH), lambda i: (prev(i), hb_blk)),
            pl.BlockSpec((HALO_ROWS, B_WIDTH), lambda i: (prev(i), gc_blk)),
            pl.BlockSpec((HALO_ROWS, B_WIDTH), lambda i: (nxt(i), hb_blk)),
            pl.BlockSpec((HALO_ROWS, B_WIDTH), lambda i: (nxt(i), gc_blk)),
            pl.BlockSpec((tm, d), lambda i: (i, 0)),
            pl.BlockSpec((1, 1, d), lambda i: (row_of(i), 0, 0)),
            full(ws), full(bias_full), full(vg), full(cw), full(w_out),
        ],
        out_specs=pl.BlockSpec((tm, d), lambda i: (i, 0)),
        out_shape=jax.ShapeDtypeStruct((n, d), F32),
        compiler_params=_cparams("parallel"),
        name="mix_ab",
    )(p, p, p, p, p, x, g1, ws, bias_full, vg, cw, w_out)


def _qk_prep_kernel(q_ref, k_ref, v_ref, cos_ref, sin_ref, qg_ref, kg_ref, *rest):
    qm_ref, ko_ref, vo_ref = rest[-3:]
    lane = lax.broadcasted_iota(jnp.int32, q_ref.shape, 1)
    first = (lane % ROPE_AXIS_DIM) < (ROPE_AXIS_DIM // 2)
    w = q_ref.shape[1]
    half = ROPE_AXIS_DIM // 2

    def prep(t, g):
        t = t * lax.rsqrt(_group_mean_sq(t, D_QK) + RMS_EPS) * g
        partner = jnp.where(first, pltpu.roll(t, w - half, 1), pltpu.roll(t, half, 1))
        return t * cos_ref[...] + partner * sin_ref[...]

    q = prep(q_ref[...].astype(F32), qg_ref[...]) * (ATTN_SCALE * math.log2(math.e))
    in_h0 = (lane % (2 * D_QK)) < D_QK
    qm_ref[0] = jnp.where(in_h0, q, 0.0).astype(BF16)
    qm_ref[1] = jnp.where(in_h0, 0.0, q).astype(BF16)
    ko_ref[...] = prep(k_ref[...].astype(F32), kg_ref[...]).astype(BF16)
    vo_ref[...] = v_ref[...]


def _qk_prep(p, cos_t, sin_t, qg, kg, tab_of, tm, kv_rows, kv_row_of, kv_bufs=None):
    n = p.shape[0]
    q_blk = HY_IN // QK_W
    in_specs = [
        pl.BlockSpec((tm, QK_W), lambda i: (i, q_blk)),
        pl.BlockSpec((tm, QK_W), lambda i: (i, q_blk + 1)),
        pl.BlockSpec((tm, V_W), lambda i: (i, q_blk + 2)),
        pl.BlockSpec((tm, QK_W), lambda i: (tab_of(i), 0)),
        pl.BlockSpec((tm, QK_W), lambda i: (tab_of(i), 0)),
        pl.BlockSpec((1, QK_W), lambda i: (0, 0)),
        pl.BlockSpec((1, QK_W), lambda i: (0, 0)),
    ]
    args = [p, p, p, cos_t, sin_t, qg, kg]
    aliases = {}
    if kv_bufs is not None:
        aliases = {len(args): 1, len(args) + 1: 2}
        in_specs += [pl.BlockSpec(memory_space=pl.ANY)] * 2
        args += list(kv_bufs)
    return pl.pallas_call(
        _qk_prep_kernel,
        grid=(n // tm,),
        in_specs=in_specs,
        out_specs=[
            pl.BlockSpec((2, tm, QK_W), lambda i: (0, i, 0)),
            pl.BlockSpec((tm, QK_W), lambda i: (kv_row_of(i), 0)),
            pl.BlockSpec((tm, V_W), lambda i: (kv_row_of(i), 0)),
        ],
        out_shape=[jax.ShapeDtypeStruct((2, n, QK_W), BF16), jax.ShapeDtypeStruct((kv_rows, QK_W), BF16),
                   jax.ShapeDtypeStruct((kv_rows, V_W), BF16)],
        input_output_aliases=aliases,
        compiler_params=_cparams("parallel"),
        name="qk_prep",
    )(*args)


def _diff_attn_kernel(q0_ref, q1_ref, k_ref, v_ref, lq1_ref, lk1_ref, lq2_ref, lk2_ref, sg_ref, o_ref, *, lam_init):
    lam = (jnp.exp(jnp.sum(lq1_ref[...] * lk1_ref[...], axis=-1, keepdims=True))
           - jnp.exp(jnp.sum(lq2_ref[...] * lk2_ref[...], axis=-1, keepdims=True)) + lam_init)
    k = k_ref[0]
    v = v_ref[0]

    def half(q):
        s = lax.dot_general(q, k, (((1,), (1,)), ((), ())), preferred_element_type=F32)
        m = jnp.max(s, axis=-1, keepdims=True)
        p = jnp.exp2(s - m)
        l = jnp.sum(p, axis=-1, keepdims=True)
        return jnp.dot(p.astype(BF16), v, preferred_element_type=F32) / l

    o = half(q0_ref[0]) - lam * half(q1_ref[0])
    ms = jnp.mean(o * o, axis=-1, keepdims=True)
    o_ref[...] = (o * lax.rsqrt(ms + RMS_EPS) * sg_ref[...]) * (1.0 - lam_init)


def _diff_attn(qm, k_all, v_all, lam_vecs, subln_g, lam_init, batch, lq, tq, lk):
    nq = lq // tq
    vec = lambda a: pl.BlockSpec(a.shape, lambda b, h, i: (0, 0))
    return pl.pallas_call(
        functools.partial(_diff_attn_kernel, lam_init=lam_init),
        grid=(batch, D_HEADS, nq),
        in_specs=[
            pl.BlockSpec((1, tq, D_V), lambda b, h, i: (0, b * nq + i, h)),
            pl.BlockSpec((1, tq, D_V), lambda b, h, i: (1, b * nq + i, h)),
            pl.BlockSpec((1, lk, D_V), lambda b, h, i: (b, 0, h)),
            pl.BlockSpec((1, lk, D_V), lambda b, h, i: (b, 0, h)),
            vec(lam_vecs[0]), vec(lam_vecs[1]), vec(lam_vecs[2]), vec(lam_vecs[3]), vec(subln_g),
        ],
        out_specs=pl.BlockSpec((tq, D_V), lambda b, h, i: (b * nq + i, h)),
        out_shape=jax.ShapeDtypeStruct((batch * lq, V_W), F32),
        compiler_params=_cparams("parallel", "parallel", "arbitrary"),
        name="diff_attn",
    )(qm, qm, k_all, v_all, *lam_vecs, subln_g)


def _hy_pre_kernel(p_ref, pp_ref, pn_ref, sw_ref, sb_ref, x0_ref, u_ref, ub_ref, *, tiles_per_seq):
    i = pl.program_id(0)
    pos = i % tiles_per_seq
    z = p_ref[...].astype(F32)
    zp = jnp.where(pos == 0, 0.0, pp_ref[...].astype(F32)[HALO_ROWS - 1:HALO_ROWS, :])
    zn = jnp.where(pos == tiles_per_seq - 1, 0.0, pn_ref[...].astype(F32)[0:1, :])
    z_up, z_dn = _shift_rows(z, zp, zn)
    c = sw_ref[0:1, :] * z_up + sw_ref[1:2, :] * z + sw_ref[2:3, :] * z_dn + sb_ref[...]
    cw = C_WIDTH
    u = c[:, cw:2 * cw] * c[:, 2 * cw:3 * cw]
    x0_ref[...] = c[:, 0:cw]
    u_ref[...] = u
    ub_ref[...] = u.astype(BF16)


def _hy_pre(p, sw, sb, seq_len, tm):
    n = p.shape[0]
    nb8 = n // HALO_ROWS
    r8 = tm // HALO_ROWS
    prev = lambda i: jnp.maximum(i * r8 - 1, 0)
    nxt = lambda i: jnp.minimum((i + 1) * r8, nb8 - 1)
    o_spec = pl.BlockSpec((tm, C_WIDTH), lambda i: (i, 0))
    return pl.pallas_call(
        functools.partial(_hy_pre_kernel, tiles_per_seq=seq_len // tm),
        grid=(n // tm,),
        in_specs=[
            pl.BlockSpec((tm, HY_IN), lambda i: (i, 0)),
            pl.BlockSpec((HALO_ROWS, HY_IN), lambda i: (prev(i), 0)),
            pl.BlockSpec((HALO_ROWS, HY_IN), lambda i: (nxt(i), 0)),
            pl.BlockSpec((3, HY_IN), lambda i: (0, 0)),
            pl.BlockSpec((1, HY_IN), lambda i: (0, 0)),
        ],
        out_specs=[o_spec, o_spec, o_spec],
        out_shape=[jax.ShapeDtypeStruct((n, C_WIDTH), F32), jax.ShapeDtypeStruct((n, C_WIDTH), F32),
                   jax.ShapeDtypeStruct((n, C_WIDTH), BF16)],
        compiler_params=_cparams("parallel"),
        name="hyena_pre",
    )(p, p, p, sw, sb)


def _hy_filter_kernel(w1_ref, b1_ref, w2_ref, b2_ref, w3_ref, b3_ref, wo_ref, fr_ref, dl_ref, o_ref):
    L = o_ref.shape[0]
    fw = w1_ref.shape[0]
    bands = (HY_EMB - 1) // 2
    row = lax.broadcasted_iota(jnp.int32, (L, fw), 0).astype(F32)
    col = lax.broadcasted_iota(jnp.int32, (L, fw), 1)
    t = row / (L - 1.0)
    ang = (2.0 * math.pi) * row / L
    step = (bands - 1 - 1e-4) / (bands - 1)
    band = jnp.where(col <= bands, col - 1, col - 1 - bands).astype(F32)
    arg = (1e-4 + band * step) * ang
    feats = jnp.where(col == 0, t, jnp.where(col <= bands, jnp.cos(arg), jnp.where(col < HY_EMB, -jnp.sin(arg), 0.0)))
    fr = fr_ref[...]
    mm = lambda a, w: jnp.dot(a, w, precision=HI, preferred_element_type=F32)
    a = jnp.sin(fr * (mm(feats, w1_ref[...]) + b1_ref[...]))
    a = jnp.sin(fr * (mm(a, w2_ref[...]) + b2_ref[...]))
    a = jnp.sin(fr * (mm(a, w3_ref[...]) + b3_ref[...]))
    hf = mm(a, wo_ref[...])
    rowc = lax.broadcasted_iota(jnp.int32, (L, C_WIDTH), 0)
    tc = rowc.astype(F32) / (L - 1.0)
    decay = jnp.exp(-tc * dl_ref[...])
    h_fwd = hf[:, :C_WIDTH] * decay
    h_bwd = jnp.where(rowc == 0, 0.0, hf[:, C_WIDTH:] * decay)
    norm = jnp.sum(jnp.abs(h_fwd) + jnp.abs(h_bwd), axis=0, keepdims=True)
    o_ref[:, :C_WIDTH] = (h_fwd / norm).astype(BF16)
    o_ref[:, C_WIDTH:] = (h_bwd / norm).astype(BF16)


def _hy_filter(L, fp, deltas):
    full = lambda a: pl.BlockSpec(a.shape, lambda: (0,) * a.ndim)
    args = (*fp, deltas)
    return pl.pallas_call(
        _hy_filter_kernel,
        in_specs=[full(a) for a in args],
        out_specs=pl.BlockSpec((L, 2 * C_WIDTH), lambda: (0, 0)),
        out_shape=jax.ShapeDtypeStruct((L, 2 * C_WIDTH), BF16),
        compiler_params=pltpu.CompilerParams(vmem_limit_bytes=VMEM_LIMIT),
        name="hyena_filter",
    )(*args)


def _mm_kernel(a_ref, b_ref, o_ref):
    o_ref[...] = jnp.dot(a_ref[...], b_ref[...], preferred_element_type=F32)


def _matmul(a, b, tm):
    m, k = a.shape
    n = b.shape[1]
    return pl.pallas_call(
        _mm_kernel,
        grid=(m // tm,),
        in_specs=[pl.BlockSpec((tm, k), lambda i: (i, 0)), pl.BlockSpec((k, n), lambda i: (0, 0))],
        out_specs=pl.BlockSpec((tm, n), lambda i: (i, 0)),
        out_shape=jax.ShapeDtypeStruct((m, n), F32),
        compiler_params=_cparams("parallel"),
        name="dft_filter",
    )(a, b)


def _hy_conv_kernel(u_ref, wt_ref, wb_ref, w2t_ref, w2b_ref, at_ref, ab_ref, y_ref):
    j = pl.program_id(1)
    cw = C_WIDTH
    u = u_ref[...]
    uc = jnp.dot(wt_ref[...], u, preferred_element_type=F32)
    us = jnp.dot(wb_ref[...], u, preferred_element_type=F32)
    kre = at_ref[:, :cw] + at_ref[:, cw:]
    kim = ab_ref[:, cw:] - ab_ref[:, :cw]
    knyq = ab_ref[:, :cw] + ab_ref[:, cw:]
    row = lax.broadcasted_iota(jnp.int32, kre.shape, 0)
    special = jnp.logical_and(row == 0, j == 0)
    p2 = jnp.where(special, 0.0, kim)
    p4 = jnp.where(special, knyq, -kre)
    yt = (uc * kre + us * p2).astype(BF16)
    yb = (uc * p2 + us * p4).astype(BF16)
    L = y_ref.shape[0]
    tr = min(L, 512)

    def inverse_rows(rows, first):
        contrib = (jnp.dot(w2t_ref[rows, :], yt, preferred_element_type=F32)
                   + jnp.dot(w2b_ref[rows, :], yb, preferred_element_type=F32))
        y_ref[rows, :] = contrib if first else y_ref[rows, :] + contrib

    @pl.when(j == 0)
    def _():
        for c in range(L // tr):
            inverse_rows(slice(c * tr, (c + 1) * tr), True)

    @pl.when(j > 0)
    def _():
        for c in range(L // tr):
            inverse_rows(slice(c * tr, (c + 1) * tr), False)


def _hy_conv(ub, w, w2, a_spec, batch, L, rb):
    ns = L // rb
    return pl.pallas_call(
        _hy_conv_kernel,
        grid=(batch, ns),
        in_specs=[
            pl.BlockSpec((L, C_WIDTH), lambda b, j: (b, 0)),
            pl.BlockSpec((rb, L), lambda b, j: (j, 0)),
            pl.BlockSpec((rb, L), lambda b, j: (j + ns, 0)),
            pl.BlockSpec((L, rb), lambda b, j: (0, j)),
            pl.BlockSpec((L, rb), lambda b, j: (0, j + ns)),
            pl.BlockSpec((rb, 2 * C_WIDTH), lambda b, j: (j, 0)),
            pl.BlockSpec((rb, 2 * C_WIDTH), lambda b, j: (j + ns, 0)),
        ],
        out_specs=pl.BlockSpec((L, C_WIDTH), lambda b, j: (b, 0)),
        out_shape=jax.ShapeDtypeStruct((batch * L, C_WIDTH), F32),
        compiler_params=_cparams("parallel", "arbitrary"),
        name="hyena_conv",
    )(ub, w, w, w2, w2, a_spec, a_spec)


def _proj_cd_kernel(x0_ref, y_ref, u_ref, skip_ref, yd_ref, w_ref, x_ref, g1_ref, o_ref):
    yh = (x0_ref[...] * (y_ref[...] + u_ref[...] * skip_ref[...])).astype(BF16)
    acc = jnp.dot(yh, w_ref[:C_WIDTH, :], preferred_element_type=F32)
    acc += jnp.dot(yd_ref[...].astype(BF16), w_ref[C_WIDTH:, :], preferred_element_type=F32)
    o_ref[...] = x_ref[...] + g1_ref[0] * acc


def _proj_cd(x0, y, u, skip, yd, w_out, x, g1, row_of, tm):
    n, d = x.shape
    half = pl.BlockSpec((tm, C_WIDTH), lambda i: (i, 0))
    return pl.pallas_call(
        _proj_cd_kernel,
        grid=(n // tm,),
        in_specs=[
            half, half, half,
            pl.BlockSpec((1, C_WIDTH), lambda i: (0, 0)),
            half,
            pl.BlockSpec(w_out.shape, lambda i: (0, 0)),
            pl.BlockSpec((tm, d), lambda i: (i, 0)),
            pl.BlockSpec((1, 1, d), lambda i: (row_of(i), 0, 0)),
        ],
        out_specs=pl.BlockSpec((tm, d), lambda i: (i, 0)),
        out_shape=jax.ShapeDtypeStruct((n, d), F32),
        compiler_params=_cparams("parallel"),
        name="proj_cd",
    )(x0, y, u, skip, yd, w_out, x, g1)


def _dft_mats(L):
    m = 2 * L
    r = jnp.arange(L, dtype=jnp.int32)
    blk = math.gcd(L, 64)
    phase = lambda cols: ((r[:, None] * cols[None, :]) % m).astype(F32) * (2.0 * math.pi / m)
    a_hi = phase(jnp.arange(L // blk, dtype=jnp.int32) * blk)[:, :, None]
    a_lo = phase(jnp.arange(blk, dtype=jnp.int32))[:, None, :]
    c = (jnp.cos(a_hi) * jnp.cos(a_lo) - jnp.sin(a_hi) * jnp.sin(a_lo)).reshape(L, L)
    s = (jnp.sin(a_hi) * jnp.cos(a_lo) + jnp.cos(a_hi) * jnp.sin(a_lo)).reshape(L, L)
    alt = jnp.where(r % 2 == 0, 1.0, -1.0).astype(F32)
    is0 = (r == 0)[:, None]
    fwd = jnp.concatenate([c, jnp.where(is0, alt[None, :], s)], axis=0)
    inv_top = jnp.where(is0, 1.0 / m, c * (2.0 / m))
    inv_bot = jnp.where(is0, alt[None, :] / m, s * (-2.0 / m))
    inv = jnp.concatenate([inv_top, inv_bot], axis=0).T
    return fwd.astype(BF16), inv.astype(BF16)


def _peer_kernel(x_ref, sc_ref, sh_ref, g2_ref, ng_ref, wqT_ref, keys_ref, u0_ref, u_ref, vT_ref, o_ref,
                 h2T_s, sT_s, tops_s, th_s, iz_s, aT_s, bT_s, thT_s, act0_s, act1_s, w0_s, w1_s,
                 cutb0_s, cutb1_s, ab0_s, ab1_s, acc_s, *, nblk):
    e = pl.program_id(1)
    nk = keys_ref.shape[1]
    T = x_ref.shape[0]
    eb = u_ref.shape[0]
    nh = PEER_HEADS
    ntc = T // LANES
    neg = -jnp.inf

    def fill_rows(cutb, ab, blk):
        base = blk * SUBLANES if isinstance(blk, int) else pl.multiple_of(blk * SUBLANES, SUBLANES)
        for h in range(nh):
            cut8 = thT_s[h, pl.ds(base, SUBLANES), :]
            a8 = aT_s[h, pl.ds(base, SUBLANES), :]
            for ii in range(SUBLANES):
                cutb[h, ii] = jnp.broadcast_to(cut8[ii:ii + 1, :], (SUBLANES, T))
                ab[h, ii] = jnp.broadcast_to(a8[ii:ii + 1, :], (SUBLANES, T))

    @pl.when(e == 0)
    def _route():
        h2 = _norm_mod(x_ref[...], ng_ref[...], sc_ref[0], sh_ref[0])
        h2T = h2.T.astype(BF16)
        h2T_s[...] = h2T
        qT = jnp.dot(wqT_ref[...], h2T, preferred_element_type=F32).astype(BF16)
        pk = keys_ref.shape[2]
        for hp in range(2 * nh):
            sT_s[hp] = jnp.dot(keys_ref[hp], qT[hp * pk:(hp + 1) * pk, :], preferred_element_type=F32)

        def exchange(v, pairs):
            for i, j in pairs:
                v[i], v[j] = jnp.maximum(v[i], v[j]), jnp.minimum(v[i], v[j])

        def top_body(hp, carry):
            for tc in range(ntc):
                sl = slice(tc * LANES, (tc + 1) * LANES)
                v = [sT_s[hp, i * SUBLANES:(i + 1) * SUBLANES, sl] for i in range(PEER_TOPK)]
                exchange(v, _SORT_PAIRS)
                shift = SUBLANES // 2
                while shift >= 1:
                    w = [pltpu.roll(a, shift, 0) for a in v]
                    v = [jnp.maximum(v[i], w[PEER_TOPK - 1 - i]) for i in range(PEER_TOPK)]
                    exchange(v, _MERGE_PAIRS)
                    shift //= 2
                for i in range(PEER_TOPK):
                    tops_s[hp, i:i + 1, sl] = v[i][0:1]
            return carry

        lax.fori_loop(0, 2 * nh, top_body, 0)

        act0_s[...] = jnp.dot(u0_ref[...], h2T, preferred_element_type=F32)

        pairs = [(i, j) for i in range(PEER_TOPK) for j in range(PEER_TOPK // (i + 1))]
        for tc in range(ntc):
            sl = slice(tc * LANES, (tc + 1) * LANES)
            A = [jnp.concatenate([tops_s[2 * h, i:i + 1, sl] for h in range(nh)], axis=0) for i in range(PEER_TOPK)]
            B = [jnp.concatenate([tops_s[2 * h + 1, j:j + 1, sl] for h in range(nh)], axis=0)
                 for j in range(PEER_TOPK)]
            cand = [A[i] + B[j] for (i, j) in pairs]
            work = list(cand)
            tau = None
            for r in range(PEER_TOPK):
                tau = functools.reduce(jnp.maximum, work)
                if r < PEER_TOPK - 1:
                    work = [jnp.where(c >= tau, neg, c) for c in work]
            m0 = cand[0]
            z = functools.reduce(lambda a, b: a + b, [jnp.where(c >= tau, jnp.exp(c - m0), 0.0) for c in cand])
            iz = 1.0 / z
            for h in range(nh):
                iz_s[h, 0:1, sl] = iz[h:h + 1]
            for i in range(PEER_TOPK):
                row = [jnp.where(cand[n] >= tau, B[j], jnp.inf) for n, (ii, j) in enumerate(pairs) if ii == i]
                cut = functools.reduce(jnp.minimum, row)
                for h in range(nh):
                    th_s[h, i:i + 1, sl] = cut[h:h + 1]

        def head_body(h, carry):
            s0 = sT_s[2 * h]
            s1 = sT_s[2 * h + 1]
            a0 = tops_s[2 * h, 0:1, :]
            b0 = tops_s[2 * h + 1, 0:1, :]
            aT_s[h] = jnp.exp(s0 - a0) * (0.5 * iz_s[h, 0:1, :])
            bT_s[h] = jnp.exp(s1 - b0)
            th = jnp.full(s0.shape, jnp.inf, F32)
            for i in range(PEER_TOPK):
                th = jnp.where(s0 == tops_s[2 * h, i:i + 1, :], th_s[h, i:i + 1, :], th)
            thT_s[h] = th
            return carry

        lax.fori_loop(0, nh, head_body, 0)

        w1_s[...] = jnp.zeros(w1_s.shape, BF16)
        acc_s[...] = jnp.zeros(acc_s.shape, F32)
        fill_rows(cutb0_s, ab0_s, 0)

    def stage(act_w, act_r, w_w, w_r, cutb_w, ab_w, cutb_r, ab_r):
        nrc = nk // GATE_ROWS
        nvr = GATE_ROWS // SUBLANES
        mrows = eb // ntc
        drows = acc_s.shape[0] // ntc
        def mxu_slice(tc):
            ru = slice(tc * mrows, (tc + 1) * mrows)
            act_w[ru, :] = jnp.dot(u_ref[ru, :], h2T_s[...], preferred_element_type=F32)
            rd = slice(tc * drows, (tc + 1) * drows)
            acc_s[rd, :] += jnp.dot(vT_ref[rd, :], w_r[...], preferred_element_type=F32)

        for tc in range(ntc):
            sl = slice(tc * LANES, (tc + 1) * LANES)
            for rc in range(nrc):
                g = [[jnp.zeros((SUBLANES, LANES), F32) for _ in range(nvr)] for _ in range(SUBLANES)]
                for h in range(nh):
                    s1c = [sT_s[2 * h + 1, rc * GATE_ROWS + v * SUBLANES:rc * GATE_ROWS + (v + 1) * SUBLANES, sl]
                           for v in range(nvr)]
                    bc = [bT_s[h, rc * GATE_ROWS + v * SUBLANES:rc * GATE_ROWS + (v + 1) * SUBLANES, sl]
                          for v in range(nvr)]
                    for ii in range(SUBLANES):
                        cut = cutb_r[h, ii, :, sl]
                        a = ab_r[h, ii, :, sl]
                        for v in range(nvr):
                            g[ii][v] = g[ii][v] + jnp.where(s1c[v] >= cut, a * bc[v], 0.0)
                for ii in range(SUBLANES):
                    rows = slice(ii * nk + rc * GATE_ROWS, ii * nk + (rc + 1) * GATE_ROWS)
                    gi = jnp.concatenate(g[ii], axis=0)
                    act = act_r[rows, sl]
                    w_w[rows, sl] = (gi * (act * (1.0 + lax.erf(act * (2.0 ** -0.5))))).astype(BF16)
            mxu_slice(tc)
        fill_rows(cutb_w, ab_w, jnp.minimum(e + 1, nblk - 1))

    @pl.when(jnp.logical_and(e < nblk, e % 2 == 0))
    def _():
        stage(act1_s, act0_s, w0_s, w1_s, cutb1_s, ab1_s, cutb0_s, ab0_s)

    @pl.when(jnp.logical_and(e < nblk, e % 2 == 1))
    def _():
        stage(act0_s, act1_s, w1_s, w0_s, cutb0_s, ab0_s, cutb1_s, ab1_s)

    @pl.when(e == nblk)
    def _():
        w_last = w0_s if (nblk - 1) % 2 == 0 else w1_s
        acc = acc_s[...] + jnp.dot(vT_ref[...], w_last[...], preferred_element_type=F32)
        o_ref[...] = x_ref[...] + g2_ref[0] * acc.T


def _peer(x, sc, sh, g2, ng, wqT, keys, u_b, vT_b, row_of, T, eb):
    n, d = x.shape
    n_exp = u_b.shape[0]
    nk = keys.shape[1]
    nh = PEER_HEADS
    modspec = pl.BlockSpec((1, 1, d), lambda i, e: (row_of(i), 0, 0))
    nblk = n_exp // eb
    assert eb == nk * SUBLANES and nk % GATE_ROWS == 0 and T % LANES == 0 and nk == SUBLANES * PEER_TOPK
    once = pl.Buffered(1)
    return pl.pallas_call(
        functools.partial(_peer_kernel, nblk=nblk),
        grid=(n // T, nblk + 1),
        in_specs=[
            pl.BlockSpec((T, d), lambda i, e: (i, 0)),
            modspec, modspec, modspec,
            pl.BlockSpec((1, d), lambda i, e: (0, 0)),
            pl.BlockSpec(wqT.shape, lambda i, e: (0, 0), pipeline_mode=once),
            pl.BlockSpec(keys.shape, lambda i, e: (0, 0, 0), pipeline_mode=once),
            pl.BlockSpec((eb, d), lambda i, e: (0, 0), pipeline_mode=once),
            pl.BlockSpec((eb, d), lambda i, e: (jnp.minimum(e + 1, nblk - 1), 0)),
            pl.BlockSpec((d, eb), lambda i, e: (0, jnp.clip(e - 1, 0, nblk - 1))),
        ],
        out_specs=pl.BlockSpec((T, d), lambda i, e: (i, 0)),
        out_shape=jax.ShapeDtypeStruct((n, d), F32),
        scratch_shapes=[
            pltpu.VMEM((d, T), BF16),
            pltpu.VMEM((2 * nh, nk, T), F32),
            pltpu.VMEM((2 * nh, PEER_TOPK, T), F32),
            pltpu.VMEM((nh, PEER_TOPK, T), F32),
            pltpu.VMEM((nh, SUBLANES, T), F32),
            pltpu.VMEM((nh, nk, T), F32),
            pltpu.VMEM((nh, nk, T), F32),
            pltpu.VMEM((nh, nk, T), F32),
            pltpu.VMEM((eb, T), F32),
            pltpu.VMEM((eb, T), F32),
            pltpu.VMEM((eb, T), BF16),
            pltpu.VMEM((eb, T), BF16),
            pltpu.VMEM((nh, SUBLANES, SUBLANES, T), F32),
            pltpu.VMEM((nh, SUBLANES, SUBLANES, T), F32),
            pltpu.VMEM((nh, SUBLANES, SUBLANES, T), F32),
            pltpu.VMEM((nh, SUBLANES, SUBLANES, T), F32),
            pltpu.VMEM((d, T), F32),
        ],
        compiler_params=_cparams("parallel", "arbitrary"),
        name="peer",
    )(x, sc, sh, g2, ng, wqT, keys, u_b, u_b, vT_b)


def _pick(n, pref):
    t = pref
    while n % t:
        t //= 2
    return t


def kernel(x, c, ctx, c_ctx, w_mod, b_mod, norm1_g, norm2_g, ab_w_in, a_ws, a_bs, a_vnorm_g, b_conv_w, ab_w_out,
           cd_w_in, c_short_w, c_short_b, c_filt_w1, c_filt_b1, c_filt_w2, c_filt_b2, c_filt_w3, c_filt_b3,
           c_filt_wout, c_filt_freq, c_skip, d_q_norm_g, d_k_norm_g, d_lambda_q1, d_lambda_k1, d_lambda_q2,
           d_lambda_k2, d_subln_g, cd_w_out, peer_wq, peer_keys, peer_u, peer_v):
    B, L, D = x.shape
    Lc = ctx.shape[1]
    depth = w_mod.shape[0]
    n_lat, n_ctx = B * L, B * Lc

    tm_l, tm_c = _pick(L, 512), _pick(Lc, 256)
    tm_kv = _pick(math.gcd(L, Lc), 256)
    tq_l, tq_c = _pick(L, 256), _pick(Lc, 256)
    tp_l, tp_c = _pick(L, 512), _pick(n_ctx, 512)
    eb = 1024

    mod_rows = 2 * SUBLANES
    s_rows = jnp.zeros((mod_rows, D), F32).at[:B].set(c).at[B].set(c_ctx)
    mod = _mod_all(s_rows, w_mod, b_mod).reshape(depth, mod_rows, N_MOD, 1, D)

    xl = x.reshape(n_lat, D)
    xc = ctx.reshape(n_ctx, D)

    def lat_row(t):
        return lambda i: (i * t) // L

    ctx_row = lambda i: B

    col = jnp.arange(QK_W)
    inv = ROPE_BASE ** (-jnp.arange(0, ROPE_AXIS_DIM, 2, dtype=F32) / ROPE_AXIS_DIM)
    pos = jnp.arange(L)
    rows_f = (pos // GRID_W).astype(F32)
    cols_f = (pos % GRID_W).astype(F32)
    is_row_part = (col % D_QK) < ROPE_AXIS_DIM
    ang = jnp.where(is_row_part[None, :], rows_f[:, None], cols_f[:, None]) * inv[col % (ROPE_AXIS_DIM // 2)][None, :]
    cos_l = jnp.cos(ang)
    sin_l = jnp.where(((col % ROPE_AXIS_DIM) < ROPE_AXIS_DIM // 2)[None, :], -jnp.sin(ang), jnp.sin(ang))
    cos_c = jnp.ones((tm_kv, QK_W), F32)
    sin_c = jnp.zeros((tm_kv, QK_W), F32)

    deltas = jnp.abs(jnp.linspace(math.log(HY_TARGET) / HY_DECAY_LONG_PCT, math.log(HY_TARGET) / HY_DECAY_SHORT_PCT,
                                  C_WIDTH, dtype=F32))[None, :]

    for l in range(depth):
        last = l == depth - 1
        odd = l % 2 == 1
        i = l // 2
        m = lambda k: mod[l, :, k]
        sh1, sc1, g1, sh2, sc2, g2 = (m(k) for k in range(N_MOD))
        n1 = norm1_g[l][None, :]
        need_ctx = (not last) or odd

        if not odd:
            w_in = ab_w_in[i].astype(BF16)
            w_out = ab_w_out[i].astype(BF16)
            ws = a_ws[i].astype(BF16)
            bias_full = jnp.repeat(a_bs[i].T, A_HEAD_DIM, axis=1)
            vg = a_vnorm_g[i][None, :]
            cw = b_conv_w[i]
            p = _norm_mod_matmul(xl, sc1, sh1, n1, w_in, lat_row(tm_l), tm_l)
            xl = _mix_ab(p, xl, g1, ws, bias_full, vg, cw, w_out, lat_row(tm_l), L, tm_l)
            if not last:
                pc = _norm_mod_matmul(xc, sc1, sh1, n1, w_in, ctx_row, tm_c)
                xc = _mix_ab(pc, xc, g1, ws, bias_full, vg, cw, w_out, ctx_row, Lc, tm_c)
        else:
            lam_init = 0.8 - 0.6 * math.exp(-0.3 * l)
            w_in = cd_w_in[i].astype(BF16)
            w_out = cd_w_out[i].astype(BF16)
            qg = jnp.tile(d_q_norm_g[i], QK_W // D_QK)[None, :]
            kg = jnp.tile(d_k_norm_g[i], QK_W // D_QK)[None, :]
            lam_vecs = tuple(v[i][None, :] for v in (d_lambda_q1, d_lambda_k1, d_lambda_q2, d_lambda_k2))
            sg = d_subln_g[i][None, :]
            fw = LANES
            padc = lambda a: jnp.pad(a, ((0, 0), (0, fw - a.shape[1])))
            padr = lambda a: jnp.pad(a, ((0, fw - a.shape[0]), (0, 0)))
            fp = (padc(padr(c_filt_w1[i])), padc(c_filt_b1[i][None, :]), padc(padr(c_filt_w2[i])),
                  padc(c_filt_b2[i][None, :]), padc(padr(c_filt_w3[i])), padc(c_filt_b3[i][None, :]),
                  padr(c_filt_wout[i]), padc(c_filt_freq[i][None, :]))
            sw, sb, skip = c_short_w[i], c_short_b[i][None, :], c_skip[i][None, :]

            def hyena(pp, batch, seq, tm):
                x0, u, ub = _hy_pre(pp, sw, sb, seq, tm)
                fwd, invm = _dft_mats(seq)
                filt = _hy_filter(seq, fp, deltas)
                spec = _matmul(fwd, filt, _pick(2 * seq, 512))
                y = _hy_conv(ub, fwd, invm, spec, batch, seq, _pick(seq, 256))
                return x0, y, u

            pc = _norm_mod_matmul(xc, sc1, sh1, n1, w_in, ctx_row, tm_c)
            kv_blocks = (Lc + L) // tm_kv
            ctx_kv_row = lambda t: (t // (Lc // tm_kv)) * kv_blocks + t % (Lc // tm_kv)
            lat_kv_row = lambda t: (t // (L // tm_kv)) * kv_blocks + Lc // tm_kv + t % (L // tm_kv)
            qmc, k_all, v_all = _qk_prep(pc, cos_c, sin_c, qg, kg, lambda t: 0, tm_kv, B * (Lc + L), ctx_kv_row)
            p = _norm_mod_matmul(xl, sc1, sh1, n1, w_in, lat_row(tm_l), tm_l)
            qm, k_all, v_all = _qk_prep(p, cos_l, sin_l, qg, kg, lambda t: t % (L // tm_kv), tm_kv, B * (Lc + L),
                                        lat_kv_row, (k_all, v_all))
            k_all = k_all.reshape(B, Lc + L, QK_W)
            v_all = v_all.reshape(B, Lc + L, V_W)
            yd = _diff_attn(qm, k_all, v_all, lam_vecs, sg, lam_init, B, L, tq_l, Lc + L)
            x0, y, u = hyena(p, B, L, tm_l)
            xl = _proj_cd(x0, y, u, skip, yd, w_out, xl, g1, lat_row(tm_l), tm_l)
            if not last:
                ydc = _diff_attn(qmc, k_all, v_all, lam_vecs, sg, lam_init, B, Lc, tq_c, Lc)
                x0c, yc, uc = hyena(pc, B, Lc, tm_c)
                xc = _proj_cd(x0c, yc, uc, skip, ydc, w_out, xc, g1, ctx_row, tm_c)

        n2 = norm2_g[l][None, :]
        wqT = peer_wq[l].T.astype(BF16)
        keys = peer_keys[l].reshape(2 * PEER_HEADS, peer_keys.shape[3], peer_keys.shape[4]).astype(BF16)
        u_b = peer_u[l].astype(BF16)
        vT_b = peer_v[l].T.astype(BF16)
        xl = _peer(xl, sc2, sh2, g2, n2, wqT, keys, u_b, vT_b, lat_row(tp_l), tp_l, eb)
        if not last:
            xc = _peer(xc, sc2, sh2, g2, n2, wqT, keys, u_b, vT_b, ctx_row, tp_c, eb)

    return xl.reshape(B, L, D)
```

```python
import functools
import math

import jax
import jax.numpy as jnp
from jax import lax
from jax.experimental import pallas as pl
from jax.experimental.pallas import tpu as pltpu

F32 = jnp.float32
BF16 = jnp.bfloat16
HI = lax.Precision.HIGHEST

GRID_W = 64
N_MOD = 6
RMS_EPS = 1e-6
CHUNK = 128
A_HEADS = 8
A_HEAD_DIM = 64
A_WIDTH = A_HEADS * A_HEAD_DIM
B_WIDTH = 512
C_WIDTH = 512
HY_IN = 3 * C_WIDTH
HY_EMB = 33
HY_DECAY_LONG_PCT = 1.5
HY_DECAY_SHORT_PCT = 0.3
HY_TARGET = 1e-2
D_HEADS = 4
D_QK = 64
D_V = 2 * D_QK
QK_W = D_HEADS * 2 * D_QK
V_W = D_HEADS * D_V
ATTN_SCALE = D_QK ** -0.5
ROPE_AXIS_DIM = D_QK // 2
ROPE_BASE = 10000.0
PEER_HEADS = 8
PEER_TOPK = 16

LANES = 128
SUBLANES = 8
VMEM_LIMIT = 56 * 1024 * 1024
GATE_ROWS = 32
HALO_ROWS = 2 * SUBLANES


def _cparams(*sem):
    return pltpu.CompilerParams(dimension_semantics=sem, vmem_limit_bytes=VMEM_LIMIT)


def _gelu(x):
    return 0.5 * x * (1.0 + lax.erf(x * (2.0 ** -0.5)))


def _oddeven_merge_sort_pairs(n):
    pairs, p = [], 1
    while p < n:
        k = p
        while k >= 1:
            for j in range(k % p, n - k, 2 * k):
                for i in range(min(k, n - j - k)):
                    if (i + j) // (2 * p) == (i + j + k) // (2 * p):
                        pairs.append((i + j, i + j + k))
            k //= 2
        p *= 2
    return pairs


def _bitonic_merge_pairs(n):
    pairs, d = [], n // 2
    while d >= 1:
        pairs += [(i, i + d) for i in range(n) if (i & d) == 0]
        d //= 2
    return pairs


_SORT_PAIRS = _oddeven_merge_sort_pairs(PEER_TOPK)
_MERGE_PAIRS = _bitonic_merge_pairs(PEER_TOPK)


def _group_mean_sq(v, group):
    w = v.shape[-1]
    r = lax.broadcasted_iota(jnp.int32, (w, w), 0) // group
    c = lax.broadcasted_iota(jnp.int32, (w, w), 1) // group
    bd = jnp.where(r == c, 1.0, 0.0).astype(BF16)
    vv = v * v
    hi = vv.astype(BF16)
    lo = (vv - hi.astype(F32)).astype(BF16)
    s = jnp.dot(hi, bd, preferred_element_type=F32) + jnp.dot(lo, bd, preferred_element_type=F32)
    return s * (1.0 / group)


def _mod_kernel(s_ref, w_ref, b_ref, o_ref):
    s = s_ref[...]
    s = s / (1.0 + jnp.exp(-s))
    o_ref[0] = jnp.dot(s, w_ref[0], precision=HI, preferred_element_type=F32) + b_ref[0]


def _mod_all(s_rows, w_mod, b_mod):
    depth, d, n6 = w_mod.shape
    rows = s_rows.shape[0]
    tn = n6 // 4
    return pl.pallas_call(
        _mod_kernel,
        grid=(depth, n6 // tn),
        in_specs=[
            pl.BlockSpec((rows, d), lambda l, j: (0, 0)),
            pl.BlockSpec((1, d, tn), lambda l, j: (l, 0, j)),
            pl.BlockSpec((1, 1, tn), lambda l, j: (l, 0, j)),
        ],
        out_specs=pl.BlockSpec((1, rows, tn), lambda l, j: (l, 0, j)),
        out_shape=jax.ShapeDtypeStruct((depth, rows, n6), F32),
        compiler_params=_cparams("arbitrary", "arbitrary"),
        name="mod",
    )(s_rows, w_mod, b_mod.reshape(depth, 1, n6))


def _norm_mod(x, g, sc, sh):
    ms = jnp.mean(x * x, axis=-1, keepdims=True)
    return (x * lax.rsqrt(ms + RMS_EPS) * g) * (1.0 + sc) + sh


def _nmm_kernel(x_ref, sc_ref, sh_ref, g_ref, w_ref, o_ref):
    h = _norm_mod(x_ref[...], g_ref[...], sc_ref[0], sh_ref[0])
    o_ref[...] = jnp.dot(h.astype(BF16), w_ref[...], preferred_element_type=F32).astype(o_ref.dtype)


def _norm_mod_matmul(x, sc, sh, g, w, row_of, tm):
    n, d = x.shape
    nout = w.shape[1]
    return pl.pallas_call(
        _nmm_kernel,
        grid=(n // tm,),
        in_specs=[
            pl.BlockSpec((tm, d), lambda i: (i, 0)),
            pl.BlockSpec((1, 1, d), lambda i: (row_of(i), 0, 0)),
            pl.BlockSpec((1, 1, d), lambda i: (row_of(i), 0, 0)),
            pl.BlockSpec((1, d), lambda i: (0, 0)),
            pl.BlockSpec((d, nout), lambda i: (0, 0)),
        ],
        out_specs=pl.BlockSpec((tm, nout), lambda i: (i, 0)),
        out_shape=jax.ShapeDtypeStruct((n, nout), BF16),
        compiler_params=_cparams("parallel"),
        name="norm_mod_inproj",
    )(x, sc, sh, g, w)


def _shift_rows(z, prev_row, next_row):
    tm = z.shape[0]
    row = lax.broadcasted_iota(jnp.int32, z.shape, 0)
    up = jnp.where(row == 0, prev_row, pltpu.roll(z, 1, 0))
    dn = jnp.where(row == tm - 1, next_row, pltpu.roll(z, tm - 1, 0))
    return up, dn


def _mix_ab_kernel(p_ref, hbp_ref, gcp_ref, hbn_ref, gcn_ref, x_ref, g1_ref, ws_ref, bias_ref, vg_ref, cw_ref,
                   wout_ref, o_ref, *, tiles_per_seq):
    i = pl.program_id(0)
    tm = p_ref.shape[0]
    aw, bw = A_WIDTH, B_WIDTH
    u = _gelu(p_ref[:, 0:aw].astype(F32))
    v = _gelu(p_ref[:, aw:2 * aw].astype(F32))
    vn = (v * lax.rsqrt(_group_mean_sq(v, A_HEAD_DIM) + RMS_EPS) * vg_ref[...]).astype(BF16)
    sv_chunks = []
    for cidx in range(tm // CHUNK):
        vc = vn[cidx * CHUNK:(cidx + 1) * CHUNK]
        heads = [jnp.dot(ws_ref[h], vc[:, h * A_HEAD_DIM:(h + 1) * A_HEAD_DIM], preferred_element_type=F32)
                 for h in range(A_HEADS)]
        sv_chunks.append(jnp.concatenate(heads, axis=-1) + bias_ref[...])
    sv = jnp.concatenate(sv_chunks, axis=0) if len(sv_chunks) > 1 else sv_chunks[0]
    ya = u * sv

    o0 = 2 * aw
    hb = p_ref[:, o0:o0 + bw].astype(F32)
    gb = p_ref[:, o0 + bw:o0 + 2 * bw].astype(F32)
    gc = p_ref[:, o0 + 2 * bw:o0 + 3 * bw].astype(F32)
    z = gc * hb
    pos = i % tiles_per_seq
    last_row = lambda r: r[...].astype(F32)[HALO_ROWS - 1:HALO_ROWS, :]
    first_row = lambda r: r[...].astype(F32)[0:1, :]
    zp = jnp.where(pos == 0, 0.0, last_row(gcp_ref) * last_row(hbp_ref))
    zn = jnp.where(pos == tiles_per_seq - 1, 0.0, first_row(gcn_ref) * first_row(hbn_ref))
    z_up, z_dn = _shift_rows(z, zp, zn)
    yb = gb * (cw_ref[0:1, :] * z_up + cw_ref[1:2, :] * z + cw_ref[2:3, :] * z_dn)

    y = jnp.concatenate([ya, yb], axis=-1).astype(BF16)
    o_ref[...] = x_ref[...] + g1_ref[0] * jnp.dot(y, wout_ref[...], preferred_element_type=F32)


def _mix_ab(p, x, g1, ws, bias_full, vg, cw, w_out, row_of, seq_len, tm):
    n, d = x.shape
    nin = p.shape[1]
    nb8 = n // HALO_ROWS
    r8 = tm // HALO_ROWS
    hb_blk = (2 * A_WIDTH) // B_WIDTH
    gc_blk = hb_blk + 2
    prev = lambda i: jnp.maximum(i * r8 - 1, 0)
    nxt = lambda i: jnp.minimum((i + 1) * r8, nb8 - 1)
    full = lambda a: pl.BlockSpec(a.shape, lambda i: (0,) * a.ndim)
    return pl.pallas_call(
        functools.partial(_mix_ab_kernel, tiles_per_seq=seq_len // tm),
        grid=(n // tm,),
        in_specs=[
            pl.BlockSpec((tm, nin), lambda i: (i, 0)),
            pl.BlockSpec((HALO_ROWS, B_WIDTH), lambda i: (prev(i), hb_blk)),
            pl.BlockSpec((HALO_ROWS, B_WIDTH), lambda i: (prev(i), gc_blk)),
            pl.BlockSpec((HALO_ROWS, B_WIDTH), lambda i: (nxt(i), hb_blk)),
            pl.BlockSpec((HALO_ROWS, B_WIDTH), lambda i: (nxt(i), gc_blk)),
            pl.BlockSpec((tm, d), lambda i: (i, 0)),
            pl.BlockSpec((1, 1, d), lambda i: (row_of(i), 0, 0)),
            full(ws), full(bias_full), full(vg), full(cw), full(w_out),
        ],
        out_specs=pl.BlockSpec((tm, d), lambda i: (i, 0)),
        out_shape=jax.ShapeDtypeStruct((n, d), F32),
        compiler_params=_cparams("parallel"),
        name="mix_ab",
    )(p, p, p, p, p, x, g1, ws, bias_full, vg, cw, w_out)


def _qk_prep_kernel(q_ref, k_ref, v_ref, cos_ref, sin_ref, qg_ref, kg_ref, *rest):
    qm_ref, ko_ref, vo_ref = rest[-3:]
    lane = lax.broadcasted_iota(jnp.int32, q_ref.shape, 1)
    first = (lane % ROPE_AXIS_DIM) < (ROPE_AXIS_DIM // 2)
    w = q_ref.shape[1]
    half = ROPE_AXIS_DIM // 2

    def prep(t, g):
        t = t * lax.rsqrt(_group_mean_sq(t, D_QK) + RMS_EPS) * g
        partner = jnp.where(first, pltpu.roll(t, w - half, 1), pltpu.roll(t, half, 1))
        return t * cos_ref[...] + partner * sin_ref[...]

    q = prep(q_ref[...].astype(F32), qg_ref[...]) * (ATTN_SCALE * math.log2(math.e))
    in_h0 = (lane % (2 * D_QK)) < D_QK
    qm_ref[0] = jnp.where(in_h0, q, 0.0).astype(BF16)
    qm_ref[1] = jnp.where(in_h0, 0.0, q).astype(BF16)
    ko_ref[...] = prep(k_ref[...].astype(F32), kg_ref[...]).astype(BF16)
    vo_ref[...] = v_ref[...]


def _qk_prep(p, cos_t, sin_t, qg, kg, tab_of, tm, kv_rows, kv_row_of, kv_bufs=None):
    n = p.shape[0]
    q_blk = HY_IN // QK_W
    in_specs = [
        pl.BlockSpec((tm, QK_W), lambda i: (i, q_blk)),
        pl.BlockSpec((tm, QK_W), lambda i: (i, q_blk + 1)),
        pl.BlockSpec((tm, V_W), lambda i: (i, q_blk + 2)),
        pl.BlockSpec((tm, QK_W), lambda i: (tab_of(i), 0)),
        pl.BlockSpec((tm, QK_W), lambda i: (tab_of(i), 0)),
        pl.BlockSpec((1, QK_W), lambda i: (0, 0)),
        pl.BlockSpec((1, QK_W), lambda i: (0, 0)),
    ]
    args = [p, p, p, cos_t, sin_t, qg, kg]
    aliases = {}
    if kv_bufs is not None:
        aliases = {len(args): 1, len(args) + 1: 2}
        in_specs += [pl.BlockSpec(memory_space=pl.ANY)] * 2
        args += list(kv_bufs)
    return pl.pallas_call(
        _qk_prep_kernel,
        grid=(n // tm,),
        in_specs=in_specs,
        out_specs=[
            pl.BlockSpec((2, tm, QK_W), lambda i: (0, i, 0)),
            pl.BlockSpec((tm, QK_W), lambda i: (kv_row_of(i), 0)),
            pl.BlockSpec((tm, V_W), lambda i: (kv_row_of(i), 0)),
        ],
        out_shape=[jax.ShapeDtypeStruct((2, n, QK_W), BF16), jax.ShapeDtypeStruct((kv_rows, QK_W), BF16),
                   jax.ShapeDtypeStruct((kv_rows, V_W), BF16)],
        input_output_aliases=aliases,
        compiler_params=_cparams("parallel"),
        name="qk_prep",
    )(*args)


def _diff_attn_kernel(q0_ref, q1_ref, k_ref, v_ref, lq1_ref, lk1_ref, lq2_ref, lk2_ref, sg_ref, o_ref, *, lam_init):
    lam = (jnp.exp(jnp.sum(lq1_ref[...] * lk1_ref[...], axis=-1, keepdims=True))
           - jnp.exp(jnp.sum(lq2_ref[...] * lk2_ref[...], axis=-1, keepdims=True)) + lam_init)
    k = k_ref[0]
    vT = v_ref[0]

    def half(q):
        sT = lax.dot_general(k, q, (((1,), (1,)), ((), ())), preferred_element_type=F32)
        m = jnp.max(sT, axis=0, keepdims=True)
        p = jnp.exp2(sT - m)
        l = jnp.sum(p, axis=0, keepdims=True)
        return jnp.dot(vT, p.astype(BF16), preferred_element_type=F32) / l

    o = (half(q0_ref[0]) - lam * half(q1_ref[0])).T
    ms = jnp.mean(o * o, axis=-1, keepdims=True)
    o_ref[...] = (o * lax.rsqrt(ms + RMS_EPS) * sg_ref[...]) * (1.0 - lam_init)


def _diff_attn(qm, k_all, vT_all, lam_vecs, subln_g, lam_init, batch, lq, tq, lk):
    nq = lq // tq
    vec = lambda a: pl.BlockSpec(a.shape, lambda b, h, i: (0, 0))
    return pl.pallas_call(
        functools.partial(_diff_attn_kernel, lam_init=lam_init),
        grid=(batch, D_HEADS, nq),
        in_specs=[
            pl.BlockSpec((1, tq, D_V), lambda b, h, i: (0, b * nq + i, h)),
            pl.BlockSpec((1, tq, D_V), lambda b, h, i: (1, b * nq + i, h)),
            pl.BlockSpec((1, lk, D_V), lambda b, h, i: (b, 0, h)),
            pl.BlockSpec((1, D_V, lk), lambda b, h, i: (b, h, 0)),
            vec(lam_vecs[0]), vec(lam_vecs[1]), vec(lam_vecs[2]), vec(lam_vecs[3]), vec(subln_g),
        ],
        out_specs=pl.BlockSpec((tq, D_V), lambda b, h, i: (b * nq + i, h)),
        out_shape=jax.ShapeDtypeStruct((batch * lq, V_W), F32),
        compiler_params=_cparams("parallel", "parallel", "arbitrary"),
        name="diff_attn",
    )(qm, qm, k_all, vT_all, *lam_vecs, subln_g)


def _hy_pre_kernel(p_ref, pp_ref, pn_ref, sw_ref, sb_ref, x0_ref, u_ref, ub_ref, *, tiles_per_seq):
    i = pl.program_id(0)
    pos = i % tiles_per_seq
    z = p_ref[...].astype(F32)
    zp = jnp.where(pos == 0, 0.0, pp_ref[...].astype(F32)[HALO_ROWS - 1:HALO_ROWS, :])
    zn = jnp.where(pos == tiles_per_seq - 1, 0.0, pn_ref[...].astype(F32)[0:1, :])
    z_up, z_dn = _shift_rows(z, zp, zn)
    c = sw_ref[0:1, :] * z_up + sw_ref[1:2, :] * z + sw_ref[2:3, :] * z_dn + sb_ref[...]
    cw = C_WIDTH
    u = c[:, cw:2 * cw] * c[:, 2 * cw:3 * cw]
    x0_ref[...] = c[:, 0:cw]
    u_ref[...] = u
    ub_ref[...] = u.astype(BF16)


def _hy_pre(p, sw, sb, seq_len, tm):
    n = p.shape[0]
    nb8 = n // HALO_ROWS
    r8 = tm // HALO_ROWS
    prev = lambda i: jnp.maximum(i * r8 - 1, 0)
    nxt = lambda i: jnp.minimum((i + 1) * r8, nb8 - 1)
    o_spec = pl.BlockSpec((tm, C_WIDTH), lambda i: (i, 0))
    return pl.pallas_call(
        functools.partial(_hy_pre_kernel, tiles_per_seq=seq_len // tm),
        grid=(n // tm,),
        in_specs=[
            pl.BlockSpec((tm, HY_IN), lambda i: (i, 0)),
            pl.BlockSpec((HALO_ROWS, HY_IN), lambda i: (prev(i), 0)),
            pl.BlockSpec((HALO_ROWS, HY_IN), lambda i: (nxt(i), 0)),
            pl.BlockSpec((3, HY_IN), lambda i: (0, 0)),
            pl.BlockSpec((1, HY_IN), lambda i: (0, 0)),
        ],
        out_specs=[o_spec, o_spec, o_spec],
        out_shape=[jax.ShapeDtypeStruct((n, C_WIDTH), F32), jax.ShapeDtypeStruct((n, C_WIDTH), F32),
                   jax.ShapeDtypeStruct((n, C_WIDTH), BF16)],
        compiler_params=_cparams("parallel"),
        name="hyena_pre",
    )(p, p, p, sw, sb)


def _hy_filter_kernel(w1_ref, b1_ref, w2_ref, b2_ref, w3_ref, b3_ref, wo_ref, fr_ref, dl_ref, o_ref):
    L = o_ref.shape[0]
    fw = w1_ref.shape[0]
    bands = (HY_EMB - 1) // 2
    row = lax.broadcasted_iota(jnp.int32, (L, fw), 0).astype(F32)
    col = lax.broadcasted_iota(jnp.int32, (L, fw), 1)
    t = row / (L - 1.0)
    ang = (2.0 * math.pi) * row / L
    step = (bands - 1 - 1e-4) / (bands - 1)
    band = jnp.where(col <= bands, col - 1, col - 1 - bands).astype(F32)
    arg = (1e-4 + band * step) * ang
    feats = jnp.where(col == 0, t, jnp.where(col <= bands, jnp.cos(arg), jnp.where(col < HY_EMB, -jnp.sin(arg), 0.0)))
    fr = fr_ref[...]
    mm = lambda a, w: jnp.dot(a, w, precision=HI, preferred_element_type=F32)
    a = jnp.sin(fr * (mm(feats, w1_ref[...]) + b1_ref[...]))
    a = jnp.sin(fr * (mm(a, w2_ref[...]) + b2_ref[...]))
    a = jnp.sin(fr * (mm(a, w3_ref[...]) + b3_ref[...]))
    hf = mm(a, wo_ref[...])
    rowc = lax.broadcasted_iota(jnp.int32, (L, C_WIDTH), 0)
    tc = rowc.astype(F32) / (L - 1.0)
    decay = jnp.exp(-tc * dl_ref[...])
    h_fwd = hf[:, :C_WIDTH] * decay
    h_bwd = jnp.where(rowc == 0, 0.0, hf[:, C_WIDTH:] * decay)
    norm = jnp.sum(jnp.abs(h_fwd) + jnp.abs(h_bwd), axis=0, keepdims=True)
    o_ref[:, :C_WIDTH] = (h_fwd / norm).astype(BF16)
    o_ref[:, C_WIDTH:] = (h_bwd / norm).astype(BF16)


def _hy_filter(L, fp, deltas):
    full = lambda a: pl.BlockSpec(a.shape, lambda: (0,) * a.ndim)
    args = (*fp, deltas)
    return pl.pallas_call(
        _hy_filter_kernel,
        in_specs=[full(a) for a in args],
        out_specs=pl.BlockSpec((L, 2 * C_WIDTH), lambda: (0, 0)),
        out_shape=jax.ShapeDtypeStruct((L, 2 * C_WIDTH), BF16),
        compiler_params=pltpu.CompilerParams(vmem_limit_bytes=VMEM_LIMIT),
        name="hyena_filter",
    )(*args)


def _mm_kernel(a_ref, b_ref, o_ref):
    o_ref[...] = jnp.dot(a_ref[...], b_ref[...], preferred_element_type=F32)


def _matmul(a, b, tm):
    m, k = a.shape
    n = b.shape[1]
    return pl.pallas_call(
        _mm_kernel,
        grid=(m // tm,),
        in_specs=[pl.BlockSpec((tm, k), lambda i: (i, 0)), pl.BlockSpec((k, n), lambda i: (0, 0))],
        out_specs=pl.BlockSpec((tm, n), lambda i: (i, 0)),
        out_shape=jax.ShapeDtypeStruct((m, n), F32),
        compiler_params=_cparams("parallel"),
        name="dft_filter",
    )(a, b)


def _hy_conv_kernel(u_ref, wt_ref, wb_ref, w2t_ref, w2b_ref, at_ref, ab_ref, y_ref):
    j = pl.program_id(1)
    cw = C_WIDTH
    u = u_ref[...]
    uc = jnp.dot(wt_ref[...], u, preferred_element_type=F32)
    us = jnp.dot(wb_ref[...], u, preferred_element_type=F32)
    kre = at_ref[:, :cw] + at_ref[:, cw:]
    kim = ab_ref[:, cw:] - ab_ref[:, :cw]
    knyq = ab_ref[:, :cw] + ab_ref[:, cw:]
    row = lax.broadcasted_iota(jnp.int32, kre.shape, 0)
    special = jnp.logical_and(row == 0, j == 0)
    p2 = jnp.where(special, 0.0, kim)
    p4 = jnp.where(special, knyq, -kre)
    yt = (uc * kre + us * p2).astype(BF16)
    yb = (uc * p2 + us * p4).astype(BF16)
    L = y_ref.shape[0]
    tr = min(L, 512)

    def inverse_rows(rows, first):
        contrib = (jnp.dot(w2t_ref[rows, :], yt, preferred_element_type=F32)
                   + jnp.dot(w2b_ref[rows, :], yb, preferred_element_type=F32))
        y_ref[rows, :] = contrib if first else y_ref[rows, :] + contrib

    @pl.when(j == 0)
    def _():
        for c in range(L // tr):
            inverse_rows(slice(c * tr, (c + 1) * tr), True)

    @pl.when(j > 0)
    def _():
        for c in range(L // tr):
            inverse_rows(slice(c * tr, (c + 1) * tr), False)


def _hy_conv(ub, w, w2, a_spec, batch, L, rb):
    ns = L // rb
    return pl.pallas_call(
        _hy_conv_kernel,
        grid=(batch, ns),
        in_specs=[
            pl.BlockSpec((L, C_WIDTH), lambda b, j: (b, 0)),
            pl.BlockSpec((rb, L), lambda b, j: (j, 0)),
            pl.BlockSpec((rb, L), lambda b, j: (j + ns, 0)),
            pl.BlockSpec((L, rb), lambda b, j: (0, j)),
            pl.BlockSpec((L, rb), lambda b, j: (0, j + ns)),
            pl.BlockSpec((rb, 2 * C_WIDTH), lambda b, j: (j, 0)),
            pl.BlockSpec((rb, 2 * C_WIDTH), lambda b, j: (j + ns, 0)),
        ],
        out_specs=pl.BlockSpec((L, C_WIDTH), lambda b, j: (b, 0)),
        out_shape=jax.ShapeDtypeStruct((batch * L, C_WIDTH), F32),
        compiler_params=_cparams("parallel", "arbitrary"),
        name="hyena_conv",
    )(ub, w, w, w2, w2, a_spec, a_spec)


def _proj_cd_kernel(x0_ref, y_ref, u_ref, skip_ref, yd_ref, w_ref, x_ref, g1_ref, o_ref):
    yh = (x0_ref[...] * (y_ref[...] + u_ref[...] * skip_ref[...])).astype(BF16)
    acc = jnp.dot(yh, w_ref[:C_WIDTH, :], preferred_element_type=F32)
    acc += jnp.dot(yd_ref[...].astype(BF16), w_ref[C_WIDTH:, :], preferred_element_type=F32)
    o_ref[...] = x_ref[...] + g1_ref[0] * acc


def _proj_cd(x0, y, u, skip, yd, w_out, x, g1, row_of, tm):
    n, d = x.shape
    half = pl.BlockSpec((tm, C_WIDTH), lambda i: (i, 0))
    return pl.pallas_call(
        _proj_cd_kernel,
        grid=(n // tm,),
        in_specs=[
            half, half, half,
            pl.BlockSpec((1, C_WIDTH), lambda i: (0, 0)),
            half,
            pl.BlockSpec(w_out.shape, lambda i: (0, 0)),
            pl.BlockSpec((tm, d), lambda i: (i, 0)),
            pl.BlockSpec((1, 1, d), lambda i: (row_of(i), 0, 0)),
        ],
        out_specs=pl.BlockSpec((tm, d), lambda i: (i, 0)),
        out_shape=jax.ShapeDtypeStruct((n, d), F32),
        compiler_params=_cparams("parallel"),
        name="proj_cd",
    )(x0, y, u, skip, yd, w_out, x, g1)


def _dft_mats(L):
    m = 2 * L
    r = jnp.arange(L, dtype=jnp.int32)
    blk = math.gcd(L, 64)
    phase = lambda cols: ((r[:, None] * cols[None, :]) % m).astype(F32) * (2.0 * math.pi / m)
    a_hi = phase(jnp.arange(L // blk, dtype=jnp.int32) * blk)[:, :, None]
    a_lo = phase(jnp.arange(blk, dtype=jnp.int32))[:, None, :]
    c = (jnp.cos(a_hi) * jnp.cos(a_lo) - jnp.sin(a_hi) * jnp.sin(a_lo)).reshape(L, L)
    s = (jnp.sin(a_hi) * jnp.cos(a_lo) + jnp.cos(a_hi) * jnp.sin(a_lo)).reshape(L, L)
    alt = jnp.where(r % 2 == 0, 1.0, -1.0).astype(F32)
    is0 = (r == 0)[:, None]
    fwd = jnp.concatenate([c, jnp.where(is0, alt[None, :], s)], axis=0)
    inv_top = jnp.where(is0, 1.0 / m, c * (2.0 / m))
    inv_bot = jnp.where(is0, alt[None, :] / m, s * (-2.0 / m))
    inv = jnp.concatenate([inv_top, inv_bot], axis=0).T
    return fwd.astype(BF16), inv.astype(BF16)


def _peer_kernel(x_ref, sc_ref, sh_ref, g2_ref, ng_ref, wqT_ref, keys_ref, u0_ref, u_ref, vT_ref, o_ref,
                 h2T_s, sT_s, tops_s, th_s, iz_s, aT_s, bT_s, thT_s, act0_s, act1_s, w0_s, w1_s,
                 cutb0_s, cutb1_s, ab0_s, ab1_s, acc_s, *, nblk):
    e = pl.program_id(1)
    nk = keys_ref.shape[1]
    T = x_ref.shape[0]
    eb = u_ref.shape[0]
    nh = PEER_HEADS
    ntc = T // LANES
    neg = -jnp.inf

    def fill_rows(cutb, ab, blk):
        base = blk * SUBLANES if isinstance(blk, int) else pl.multiple_of(blk * SUBLANES, SUBLANES)
        for h in range(nh):
            cut8 = thT_s[h, pl.ds(base, SUBLANES), :]
            a8 = aT_s[h, pl.ds(base, SUBLANES), :]
            for ii in range(SUBLANES):
                cutb[h, ii] = jnp.broadcast_to(cut8[ii:ii + 1, :], (SUBLANES, T))
                ab[h, ii] = jnp.broadcast_to(a8[ii:ii + 1, :], (SUBLANES, T))

    @pl.when(e == 0)
    def _route():
        h2 = _norm_mod(x_ref[...], ng_ref[...], sc_ref[0], sh_ref[0])
        h2T = h2.T.astype(BF16)
        h2T_s[...] = h2T
        qT = jnp.dot(wqT_ref[...], h2T, preferred_element_type=F32).astype(BF16)
        pk = keys_ref.shape[2]
        for hp in range(2 * nh):
            sT_s[hp] = jnp.dot(keys_ref[hp], qT[hp * pk:(hp + 1) * pk, :], preferred_element_type=F32)

        def exchange(v, pairs):
            for i, j in pairs:
                v[i], v[j] = jnp.maximum(v[i], v[j]), jnp.minimum(v[i], v[j])

        def top_body(hp, carry):
            for tc in range(ntc):
                sl = slice(tc * LANES, (tc + 1) * LANES)
                v = [sT_s[hp, i * SUBLANES:(i + 1) * SUBLANES, sl] for i in range(PEER_TOPK)]
                exchange(v, _SORT_PAIRS)
                shift = SUBLANES // 2
                while shift >= 1:
                    w = [pltpu.roll(a, shift, 0) for a in v]
                    v = [jnp.maximum(v[i], w[PEER_TOPK - 1 - i]) for i in range(PEER_TOPK)]
                    exchange(v, _MERGE_PAIRS)
                    shift //= 2
                for i in range(PEER_TOPK):
                    tops_s[hp, i:i + 1, sl] = v[i][0:1]
            return carry

        lax.fori_loop(0, 2 * nh, top_body, 0)

        act0_s[...] = jnp.dot(u0_ref[...], h2T, preferred_element_type=F32)

        pairs = [(i, j) for i in range(PEER_TOPK) for j in range(PEER_TOPK // (i + 1))]
        for tc in range(ntc):
            sl = slice(tc * LANES, (tc + 1) * LANES)
            A = [jnp.concatenate([tops_s[2 * h, i:i + 1, sl] for h in range(nh)], axis=0) for i in range(PEER_TOPK)]
            B = [jnp.concatenate([tops_s[2 * h + 1, j:j + 1, sl] for h in range(nh)], axis=0)
                 for j in range(PEER_TOPK)]
            cand = [A[i] + B[j] for (i, j) in pairs]
            work = list(cand)
            tau = None
            for r in range(PEER_TOPK):
                tau = functools.reduce(jnp.maximum, work)
                if r < PEER_TOPK - 1:
                    work = [jnp.where(c >= tau, neg, c) for c in work]
            m0 = cand[0]
            z = functools.reduce(lambda a, b: a + b, [jnp.where(c >= tau, jnp.exp(c - m0), 0.0) for c in cand])
            iz = 1.0 / z
            for h in range(nh):
                iz_s[h, 0:1, sl] = iz[h:h + 1]
            for i in range(PEER_TOPK):
                row = [jnp.where(cand[n] >= tau, B[j], jnp.inf) for n, (ii, j) in enumerate(pairs) if ii == i]
                cut = functools.reduce(jnp.minimum, row)
                for h in range(nh):
                    th_s[h, i:i + 1, sl] = cut[h:h + 1]

        def head_body(h, carry):
            s0 = sT_s[2 * h]
            s1 = sT_s[2 * h + 1]
            a0 = tops_s[2 * h, 0:1, :]
            b0 = tops_s[2 * h + 1, 0:1, :]
            aT_s[h] = jnp.exp(s0 - a0) * (0.5 * iz_s[h, 0:1, :])
            bT_s[h] = jnp.exp(s1 - b0)
            th = jnp.full(s0.shape, jnp.inf, F32)
            for i in range(PEER_TOPK):
                th = jnp.where(s0 == tops_s[2 * h, i:i + 1, :], th_s[h, i:i + 1, :], th)
            thT_s[h] = th
            return carry

        lax.fori_loop(0, nh, head_body, 0)

        w1_s[...] = jnp.zeros(w1_s.shape, BF16)
        acc_s[...] = jnp.zeros(acc_s.shape, F32)
        fill_rows(cutb0_s, ab0_s, 0)

    def stage(act_w, act_r, w_w, w_r, cutb_w, ab_w, cutb_r, ab_r):
        nrc = nk // GATE_ROWS
        nvr = GATE_ROWS // SUBLANES
        mrows = eb // ntc
        drows = acc_s.shape[0] // ntc
        def mxu_slice(tc):
            ru = slice(tc * mrows, (tc + 1) * mrows)
            act_w[ru, :] = jnp.dot(u_ref[ru, :], h2T_s[...], preferred_element_type=F32)
            rd = slice(tc * drows, (tc + 1) * drows)
            acc_s[rd, :] += jnp.dot(vT_ref[rd, :], w_r[...], preferred_element_type=F32)

        for tc in range(ntc):
            sl = slice(tc * LANES, (tc + 1) * LANES)
            for rc in range(nrc):
                g = [[jnp.zeros((SUBLANES, LANES), F32) for _ in range(nvr)] for _ in range(SUBLANES)]
                for h in range(nh):
                    s1c = [sT_s[2 * h + 1, rc * GATE_ROWS + v * SUBLANES:rc * GATE_ROWS + (v + 1) * SUBLANES, sl]
                           for v in range(nvr)]
                    bc = [bT_s[h, rc * GATE_ROWS + v * SUBLANES:rc * GATE_ROWS + (v + 1) * SUBLANES, sl]
                          for v in range(nvr)]
                    for ii in range(SUBLANES):
                        cut = cutb_r[h, ii, :, sl]
                        a = ab_r[h, ii, :, sl]
                        for v in range(nvr):
                            g[ii][v] = g[ii][v] + jnp.where(s1c[v] >= cut, a * bc[v], 0.0)
                for ii in range(SUBLANES):
                    rows = slice(ii * nk + rc * GATE_ROWS, ii * nk + (rc + 1) * GATE_ROWS)
                    gi = jnp.concatenate(g[ii], axis=0)
                    act = act_r[rows, sl]
                    w_w[rows, sl] = (gi * (act * (1.0 + lax.erf(act * (2.0 ** -0.5))))).astype(BF16)
            mxu_slice(tc)
        fill_rows(cutb_w, ab_w, jnp.minimum(e + 1, nblk - 1))

    @pl.when(jnp.logical_and(e < nblk, e % 2 == 0))
    def _():
        stage(act1_s, act0_s, w0_s, w1_s, cutb1_s, ab1_s, cutb0_s, ab0_s)

    @pl.when(jnp.logical_and(e < nblk, e % 2 == 1))
    def _():
        stage(act0_s, act1_s, w1_s, w0_s, cutb0_s, ab0_s, cutb1_s, ab1_s)

    @pl.when(e == nblk)
    def _():
        w_last = w0_s if (nblk - 1) % 2 == 0 else w1_s
        acc = acc_s[...] + jnp.dot(vT_ref[...], w_last[...], preferred_element_type=F32)
        o_ref[...] = x_ref[...] + g2_ref[0] * acc.T


def _peer(x, sc, sh, g2, ng, wqT, keys, u_b, vT_b, row_of, T, eb):
    n, d = x.shape
    n_exp = u_b.shape[0]
    nk = keys.shape[1]
    nh = PEER_HEADS
    modspec = pl.BlockSpec((1, 1, d), lambda i, e: (row_of(i), 0, 0))
    nblk = n_exp // eb
    assert eb == nk * SUBLANES and nk % GATE_ROWS == 0 and T % LANES == 0 and nk == SUBLANES * PEER_TOPK
    once = pl.Buffered(1)
    return pl.pallas_call(
        functools.partial(_peer_kernel, nblk=nblk),
        grid=(n // T, nblk + 1),
        in_specs=[
            pl.BlockSpec((T, d), lambda i, e: (i, 0)),
            modspec, modspec, modspec,
            pl.BlockSpec((1, d), lambda i, e: (0, 0)),
            pl.BlockSpec(wqT.shape, lambda i, e: (0, 0), pipeline_mode=once),
            pl.BlockSpec(keys.shape, lambda i, e: (0, 0, 0), pipeline_mode=once),
            pl.BlockSpec((eb, d), lambda i, e: (0, 0), pipeline_mode=once),
            pl.BlockSpec((eb, d), lambda i, e: (jnp.minimum(e + 1, nblk - 1), 0)),
            pl.BlockSpec((d, eb), lambda i, e: (0, jnp.clip(e - 1, 0, nblk - 1))),
        ],
        out_specs=pl.BlockSpec((T, d), lambda i, e: (i, 0)),
        out_shape=jax.ShapeDtypeStruct((n, d), F32),
        scratch_shapes=[
            pltpu.VMEM((d, T), BF16),
            pltpu.VMEM((2 * nh, nk, T), F32),
            pltpu.VMEM((2 * nh, PEER_TOPK, T), F32),
            pltpu.VMEM((nh, PEER_TOPK, T), F32),
            pltpu.VMEM((nh, SUBLANES, T), F32),
            pltpu.VMEM((nh, nk, T), F32),
            pltpu.VMEM((nh, nk, T), F32),
            pltpu.VMEM((nh, nk, T), F32),
            pltpu.VMEM((eb, T), F32),
            pltpu.VMEM((eb, T), F32),
            pltpu.VMEM((eb, T), BF16),
            pltpu.VMEM((eb, T), BF16),
            pltpu.VMEM((nh, SUBLANES, SUBLANES, T), F32),
            pltpu.VMEM((nh, SUBLANES, SUBLANES, T), F32),
            pltpu.VMEM((nh, SUBLANES, SUBLANES, T), F32),
            pltpu.VMEM((nh, SUBLANES, SUBLANES, T), F32),
            pltpu.VMEM((d, T), F32),
        ],
        compiler_params=_cparams("parallel", "arbitrary"),
        name="peer",
    )(x, sc, sh, g2, ng, wqT, keys, u_b, u_b, vT_b)


def _pick(n, pref):
    t = pref
    while n % t:
        t //= 2
    return t


def kernel(x, c, ctx, c_ctx, w_mod, b_mod, norm1_g, norm2_g, ab_w_in, a_ws, a_bs, a_vnorm_g, b_conv_w, ab_w_out,
           cd_w_in, c_short_w, c_short_b, c_filt_w1, c_filt_b1, c_filt_w2, c_filt_b2, c_filt_w3, c_filt_b3,
           c_filt_wout, c_filt_freq, c_skip, d_q_norm_g, d_k_norm_g, d_lambda_q1, d_lambda_k1, d_lambda_q2,
           d_lambda_k2, d_subln_g, cd_w_out, peer_wq, peer_keys, peer_u, peer_v):
    B, L, D = x.shape
    Lc = ctx.shape[1]
    depth = w_mod.shape[0]
    n_lat, n_ctx = B * L, B * Lc

    tm_l, tm_c = _pick(L, 512), _pick(Lc, 256)
    tm_kv = _pick(math.gcd(L, Lc), 256)
    tq_l, tq_c = _pick(L, 512), _pick(Lc, 512)
    tp_l, tp_c = _pick(L, 512), _pick(n_ctx, 512)
    eb = 1024

    mod_rows = 2 * SUBLANES
    s_rows = jnp.zeros((mod_rows, D), F32).at[:B].set(c).at[B].set(c_ctx)
    mod = _mod_all(s_rows, w_mod, b_mod).reshape(depth, mod_rows, N_MOD, 1, D)

    xl = x.reshape(n_lat, D)
    xc = ctx.reshape(n_ctx, D)

    def lat_row(t):
        return lambda i: (i * t) // L

    ctx_row = lambda i: B

    col = jnp.arange(QK_W)
    inv = ROPE_BASE ** (-jnp.arange(0, ROPE_AXIS_DIM, 2, dtype=F32) / ROPE_AXIS_DIM)
    pos = jnp.arange(L)
    rows_f = (pos // GRID_W).astype(F32)
    cols_f = (pos % GRID_W).astype(F32)
    is_row_part = (col % D_QK) < ROPE_AXIS_DIM
    ang = jnp.where(is_row_part[None, :], rows_f[:, None], cols_f[:, None]) * inv[col % (ROPE_AXIS_DIM // 2)][None, :]
    cos_l = jnp.cos(ang)
    sin_l = jnp.where(((col % ROPE_AXIS_DIM) < ROPE_AXIS_DIM // 2)[None, :], -jnp.sin(ang), jnp.sin(ang))
    cos_c = jnp.ones((tm_kv, QK_W), F32)
    sin_c = jnp.zeros((tm_kv, QK_W), F32)

    deltas = jnp.abs(jnp.linspace(math.log(HY_TARGET) / HY_DECAY_LONG_PCT, math.log(HY_TARGET) / HY_DECAY_SHORT_PCT,
                                  C_WIDTH, dtype=F32))[None, :]

    for l in range(depth):
        last = l == depth - 1
        odd = l % 2 == 1
        i = l // 2
        m = lambda k: mod[l, :, k]
        sh1, sc1, g1, sh2, sc2, g2 = (m(k) for k in range(N_MOD))
        n1 = norm1_g[l][None, :]
        need_ctx = (not last) or odd

        if not odd:
            w_in = ab_w_in[i].astype(BF16)
            w_out = ab_w_out[i].astype(BF16)
            ws = a_ws[i].astype(BF16)
            bias_full = jnp.repeat(a_bs[i].T, A_HEAD_DIM, axis=1)
            vg = a_vnorm_g[i][None, :]
            cw = b_conv_w[i]
            p = _norm_mod_matmul(xl, sc1, sh1, n1, w_in, lat_row(tm_l), tm_l)
            xl = _mix_ab(p, xl, g1, ws, bias_full, vg, cw, w_out, lat_row(tm_l), L, tm_l)
            if not last:
                pc = _norm_mod_matmul(xc, sc1, sh1, n1, w_in, ctx_row, tm_c)
                xc = _mix_ab(pc, xc, g1, ws, bias_full, vg, cw, w_out, ctx_row, Lc, tm_c)
        else:
            lam_init = 0.8 - 0.6 * math.exp(-0.3 * l)
            w_in = cd_w_in[i].astype(BF16)
            w_out = cd_w_out[i].astype(BF16)
            qg = jnp.tile(d_q_norm_g[i], QK_W // D_QK)[None, :]
            kg = jnp.tile(d_k_norm_g[i], QK_W // D_QK)[None, :]
            lam_vecs = tuple(v[i][None, :] for v in (d_lambda_q1, d_lambda_k1, d_lambda_q2, d_lambda_k2))
            sg = d_subln_g[i][None, :]
            fw = LANES
            padc = lambda a: jnp.pad(a, ((0, 0), (0, fw - a.shape[1])))
            padr = lambda a: jnp.pad(a, ((0, fw - a.shape[0]), (0, 0)))
            fp = (padc(padr(c_filt_w1[i])), padc(c_filt_b1[i][None, :]), padc(padr(c_filt_w2[i])),
                  padc(c_filt_b2[i][None, :]), padc(padr(c_filt_w3[i])), padc(c_filt_b3[i][None, :]),
                  padr(c_filt_wout[i]), padc(c_filt_freq[i][None, :]))
            sw, sb, skip = c_short_w[i], c_short_b[i][None, :], c_skip[i][None, :]

            def hyena(pp, batch, seq, tm):
                x0, u, ub = _hy_pre(pp, sw, sb, seq, tm)
                fwd, invm = _dft_mats(seq)
                filt = _hy_filter(seq, fp, deltas)
                spec = _matmul(fwd, filt, _pick(2 * seq, 512))
                y = _hy_conv(ub, fwd, invm, spec, batch, seq, _pick(seq, 256))
                return x0, y, u

            pc = _norm_mod_matmul(xc, sc1, sh1, n1, w_in, ctx_row, tm_c)
            kv_blocks = (Lc + L) // tm_kv
            ctx_kv_row = lambda t: (t // (Lc // tm_kv)) * kv_blocks + t % (Lc // tm_kv)
            lat_kv_row = lambda t: (t // (L // tm_kv)) * kv_blocks + Lc // tm_kv + t % (L // tm_kv)
            qmc, k_all, v_all = _qk_prep(pc, cos_c, sin_c, qg, kg, lambda t: 0, tm_kv, B * (Lc + L), ctx_kv_row)
            p = _norm_mod_matmul(xl, sc1, sh1, n1, w_in, lat_row(tm_l), tm_l)
            qm, k_all, v_all = _qk_prep(p, cos_l, sin_l, qg, kg, lambda t: t % (L // tm_kv), tm_kv, B * (Lc + L),
                                        lat_kv_row, (k_all, v_all))
            k_all = k_all.reshape(B, Lc + L, QK_W)
            vT_all = v_all.reshape(B, Lc + L, V_W).transpose(0, 2, 1)
            yd = _diff_attn(qm, k_all, vT_all, lam_vecs, sg, lam_init, B, L, tq_l, Lc + L)
            x0, y, u = hyena(p, B, L, tm_l)
            xl = _proj_cd(x0, y, u, skip, yd, w_out, xl, g1, lat_row(tm_l), tm_l)
            if not last:
                ydc = _diff_attn(qmc, k_all, vT_all, lam_vecs, sg, lam_init, B, Lc, tq_c, Lc)
                x0c, yc, uc = hyena(pc, B, Lc, tm_c)
                xc = _proj_cd(x0c, yc, uc, skip, ydc, w_out, xc, g1, ctx_row, tm_c)

        n2 = norm2_g[l][None, :]
        wqT = peer_wq[l].T.astype(BF16)
        keys = peer_keys[l].reshape(2 * PEER_HEADS, peer_keys.shape[3], peer_keys.shape[4]).astype(BF16)
        u_b = peer_u[l].astype(BF16)
        vT_b = peer_v[l].T.astype(BF16)
        xl = _peer(xl, sc2, sh2, g2, n2, wqT, keys, u_b, vT_b, lat_row(tp_l), tp_l, eb)
        if not last:
            xc = _peer(xc, sc2, sh2, g2, n2, wqT, keys, u_b, vT_b, ctx_row, tp_c, eb)

    return xl.reshape(B, L, D)
```

```python
import functools
import math

import jax
import jax.numpy as jnp
from jax import lax
from jax.experimental import pallas as pl
from jax.experimental.pallas import tpu as pltpu

F32 = jnp.float32
BF16 = jnp.bfloat16
HI = lax.Precision.HIGHEST

GRID_W = 64
N_MOD = 6
RMS_EPS = 1e-6
CHUNK = 128
A_HEADS = 8
A_HEAD_DIM = 64
A_WIDTH = A_HEADS * A_HEAD_DIM
B_WIDTH = 512
C_WIDTH = 512
HY_IN = 3 * C_WIDTH
HY_EMB = 33
HY_DECAY_LONG_PCT = 1.5
HY_DECAY_SHORT_PCT = 0.3
HY_TARGET = 1e-2
D_HEADS = 4
D_QK = 64
D_V = 2 * D_QK
QK_W = D_HEADS * 2 * D_QK
V_W = D_HEADS * D_V
ATTN_SCALE = D_QK ** -0.5
ROPE_AXIS_DIM = D_QK // 2
ROPE_BASE = 10000.0
PEER_HEADS = 8
PEER_TOPK = 16

LANES = 128
SUBLANES = 8
VMEM_LIMIT = 56 * 1024 * 1024
GATE_ROWS = 32
HALO_ROWS = 2 * SUBLANES


def _cparams(*sem):
    return pltpu.CompilerParams(dimension_semantics=sem, vmem_limit_bytes=VMEM_LIMIT)


def _gelu(x):
    return 0.5 * x * (1.0 + lax.erf(x * (2.0 ** -0.5)))


def _oddeven_merge_sort_pairs(n):
    pairs, p = [], 1
    while p < n:
        k = p
        while k >= 1:
            for j in range(k % p, n - k, 2 * k):
                for i in range(min(k, n - j - k)):
                    if (i + j) // (2 * p) == (i + j + k) // (2 * p):
                        pairs.append((i + j, i + j + k))
            k //= 2
        p *= 2
    return pairs


def _bitonic_merge_pairs(n):
    pairs, d = [], n // 2
    while d >= 1:
        pairs += [(i, i + d) for i in range(n) if (i & d) == 0]
        d //= 2
    return pairs


_SORT_PAIRS = _oddeven_merge_sort_pairs(PEER_TOPK)
_MERGE_PAIRS = _bitonic_merge_pairs(PEER_TOPK)


def _group_mean_sq(v, group):
    w = v.shape[-1]
    r = lax.broadcasted_iota(jnp.int32, (w, w), 0) // group
    c = lax.broadcasted_iota(jnp.int32, (w, w), 1) // group
    bd = jnp.where(r == c, 1.0, 0.0).astype(BF16)
    vv = v * v
    hi = vv.astype(BF16)
    lo = (vv - hi.astype(F32)).astype(BF16)
    s = jnp.dot(hi, bd, preferred_element_type=F32) + jnp.dot(lo, bd, preferred_element_type=F32)
    return s * (1.0 / group)


def _mod_kernel(s_ref, w_ref, b_ref, o_ref):
    s = s_ref[...]
    s = s / (1.0 + jnp.exp(-s))
    o_ref[0] = jnp.dot(s, w_ref[0], precision=HI, preferred_element_type=F32) + b_ref[0]


def _mod_all(s_rows, w_mod, b_mod):
    depth, d, n6 = w_mod.shape
    rows = s_rows.shape[0]
    tn = n6 // 4
    return pl.pallas_call(
        _mod_kernel,
        grid=(depth, n6 // tn),
        in_specs=[
            pl.BlockSpec((rows, d), lambda l, j: (0, 0)),
            pl.BlockSpec((1, d, tn), lambda l, j: (l, 0, j)),
            pl.BlockSpec((1, 1, tn), lambda l, j: (l, 0, j)),
        ],
        out_specs=pl.BlockSpec((1, rows, tn), lambda l, j: (l, 0, j)),
        out_shape=jax.ShapeDtypeStruct((depth, rows, n6), F32),
        compiler_params=_cparams("arbitrary", "arbitrary"),
        name="mod",
    )(s_rows, w_mod, b_mod.reshape(depth, 1, n6))


def _norm_mod(x, g, sc, sh):
    ms = jnp.mean(x * x, axis=-1, keepdims=True)
    return (x * lax.rsqrt(ms + RMS_EPS) * g) * (1.0 + sc) + sh


def _nmm_kernel(x_ref, sc_ref, sh_ref, g_ref, w_ref, o_ref):
    h = _norm_mod(x_ref[...], g_ref[...], sc_ref[0], sh_ref[0])
    o_ref[...] = jnp.dot(h.astype(BF16), w_ref[...], preferred_element_type=F32).astype(o_ref.dtype)


def _norm_mod_matmul(x, sc, sh, g, w, row_of, tm):
    n, d = x.shape
    nout = w.shape[1]
    return pl.pallas_call(
        _nmm_kernel,
        grid=(n // tm,),
        in_specs=[
            pl.BlockSpec((tm, d), lambda i: (i, 0)),
            pl.BlockSpec((1, 1, d), lambda i: (row_of(i), 0, 0)),
            pl.BlockSpec((1, 1, d), lambda i: (row_of(i), 0, 0)),
            pl.BlockSpec((1, d), lambda i: (0, 0)),
            pl.BlockSpec((d, nout), lambda i: (0, 0)),
        ],
        out_specs=pl.BlockSpec((tm, nout), lambda i: (i, 0)),
        out_shape=jax.ShapeDtypeStruct((n, nout), BF16),
        compiler_params=_cparams("parallel"),
        name="norm_mod_inproj",
    )(x, sc, sh, g, w)


def _shift_rows(z, prev_row, next_row):
    tm = z.shape[0]
    row = lax.broadcasted_iota(jnp.int32, z.shape, 0)
    up = jnp.where(row == 0, prev_row, pltpu.roll(z, 1, 0))
    dn = jnp.where(row == tm - 1, next_row, pltpu.roll(z, tm - 1, 0))
    return up, dn


def _mix_ab_kernel(p_ref, hbp_ref, gcp_ref, hbn_ref, gcn_ref, x_ref, g1_ref, ws_ref, bias_ref, vg_ref, cw_ref,
                   wout_ref, o_ref, *, tiles_per_seq):
    i = pl.program_id(0)
    tm = p_ref.shape[0]
    aw, bw = A_WIDTH, B_WIDTH
    u = _gelu(p_ref[:, 0:aw].astype(F32))
    v = _gelu(p_ref[:, aw:2 * aw].astype(F32))
    vn = (v * lax.rsqrt(_group_mean_sq(v, A_HEAD_DIM) + RMS_EPS) * vg_ref[...]).astype(BF16)
    sv_chunks = []
    for cidx in range(tm // CHUNK):
        vc = vn[cidx * CHUNK:(cidx + 1) * CHUNK]
        heads = [jnp.dot(ws_ref[h], vc[:, h * A_HEAD_DIM:(h + 1) * A_HEAD_DIM], preferred_element_type=F32)
                 for h in range(A_HEADS)]
        sv_chunks.append(jnp.concatenate(heads, axis=-1) + bias_ref[...])
    sv = jnp.concatenate(sv_chunks, axis=0) if len(sv_chunks) > 1 else sv_chunks[0]
    ya = u * sv

    o0 = 2 * aw
    hb = p_ref[:, o0:o0 + bw].astype(F32)
    gb = p_ref[:, o0 + bw:o0 + 2 * bw].astype(F32)
    gc = p_ref[:, o0 + 2 * bw:o0 + 3 * bw].astype(F32)
    z = gc * hb
    pos = i % tiles_per_seq
    last_row = lambda r: r[...].astype(F32)[HALO_ROWS - 1:HALO_ROWS, :]
    first_row = lambda r: r[...].astype(F32)[0:1, :]
    zp = jnp.where(pos == 0, 0.0, last_row(gcp_ref) * last_row(hbp_ref))
    zn = jnp.where(pos == tiles_per_seq - 1, 0.0, first_row(gcn_ref) * first_row(hbn_ref))
    z_up, z_dn = _shift_rows(z, zp, zn)
    yb = gb * (cw_ref[0:1, :] * z_up + cw_ref[1:2, :] * z + cw_ref[2:3, :] * z_dn)

    y = jnp.concatenate([ya, yb], axis=-1).astype(BF16)
    o_ref[...] = x_ref[...] + g1_ref[0] * jnp.dot(y, wout_ref[...], preferred_element_type=F32)


def _mix_ab(p, x, g1, ws, bias_full, vg, cw, w_out, row_of, seq_len, tm):
    n, d = x.shape
    nin = p.shape[1]
    nb8 = n // HALO_ROWS
    r8 = tm // HALO_ROWS
    hb_blk = (2 * A_WIDTH) // B_WIDTH
    gc_blk = hb_blk + 2
    prev = lambda i: jnp.maximum(i * r8 - 1, 0)
    nxt = lambda i: jnp.minimum((i + 1) * r8, nb8 - 1)
    full = lambda a: pl.BlockSpec(a.shape, lambda i: (0,) * a.ndim)
    return pl.pallas_call(
        functools.partial(_mix_ab_kernel, tiles_per_seq=seq_len // tm),
        grid=(n // tm,),
        in_specs=[
            pl.BlockSpec((tm, nin), lambda i: (i, 0)),
            pl.BlockSpec((HALO_ROWS, B_WIDTH), lambda i: (prev(i), hb_blk)),
            pl.BlockSpec((HALO_ROWS, B_WIDTH), lambda i: (prev(i), gc_blk)),
            pl.BlockSpec((HALO_ROWS, B_WIDTH), lambda i: (nxt(i), hb_blk)),
            pl.BlockSpec((HALO_ROWS, B_WIDTH), lambda i: (nxt(i), gc_blk)),
            pl.BlockSpec((tm, d), lambda i: (i, 0)),
            pl.BlockSpec((1, 1, d), lambda i: (row_of(i), 0, 0)),
            full(ws), full(bias_full), full(vg), full(cw), full(w_out),
        ],
        out_specs=pl.BlockSpec((tm, d), lambda i: (i, 0)),
        out_shape=jax.ShapeDtypeStruct((n, d), F32),
        compiler_params=_cparams("parallel"),
        name="mix_ab",
    )(p, p, p, p, p, x, g1, ws, bias_full, vg, cw, w_out)


def _qk_prep_kernel(q_ref, k_ref, v_ref, cos_ref, sin_ref, qg_ref, kg_ref, *rest):
    qm_ref, ko_ref, vo_ref = rest[-3:]
    lane = lax.broadcasted_iota(jnp.int32, q_ref.shape, 1)
    first = (lane % ROPE_AXIS_DIM) < (ROPE_AXIS_DIM // 2)
    w = q_ref.shape[1]
    half = ROPE_AXIS_DIM // 2

    def prep(t, g):
        t = t * lax.rsqrt(_group_mean_sq(t, D_QK) + RMS_EPS) * g
        partner = jnp.where(first, pltpu.roll(t, w - half, 1), pltpu.roll(t, half, 1))
        return t * cos_ref[...] + partner * sin_ref[...]

    q = prep(q_ref[...].astype(F32), qg_ref[...]) * (ATTN_SCALE * math.log2(math.e))
    in_h0 = (lane % (2 * D_QK)) < D_QK
    qm_ref[0] = jnp.where(in_h0, q, 0.0).astype(BF16)
    qm_ref[1] = jnp.where(in_h0, 0.0, q).astype(BF16)
    ko_ref[...] = prep(k_ref[...].astype(F32), kg_ref[...]).astype(BF16)
    vo_ref[...] = v_ref[...]


def _qk_prep(p, cos_t, sin_t, qg, kg, tab_of, tm, kv_rows, kv_row_of, kv_bufs=None):
    n = p.shape[0]
    q_blk = HY_IN // QK_W
    in_specs = [
        pl.BlockSpec((tm, QK_W), lambda i: (i, q_blk)),
        pl.BlockSpec((tm, QK_W), lambda i: (i, q_blk + 1)),
        pl.BlockSpec((tm, V_W), lambda i: (i, q_blk + 2)),
        pl.BlockSpec((tm, QK_W), lambda i: (tab_of(i), 0)),
        pl.BlockSpec((tm, QK_W), lambda i: (tab_of(i), 0)),
        pl.BlockSpec((1, QK_W), lambda i: (0, 0)),
        pl.BlockSpec((1, QK_W), lambda i: (0, 0)),
    ]
    args = [p, p, p, cos_t, sin_t, qg, kg]
    aliases = {}
    if kv_bufs is not None:
        aliases = {len(args): 1, len(args) + 1: 2}
        in_specs += [pl.BlockSpec(memory_space=pl.ANY)] * 2
        args += list(kv_bufs)
    return pl.pallas_call(
        _qk_prep_kernel,
        grid=(n // tm,),
        in_specs=in_specs,
        out_specs=[
            pl.BlockSpec((2, tm, QK_W), lambda i: (0, i, 0)),
            pl.BlockSpec((tm, QK_W), lambda i: (kv_row_of(i), 0)),
            pl.BlockSpec((tm, V_W), lambda i: (kv_row_of(i), 0)),
        ],
        out_shape=[jax.ShapeDtypeStruct((2, n, QK_W), BF16), jax.ShapeDtypeStruct((kv_rows, QK_W), BF16),
                   jax.ShapeDtypeStruct((kv_rows, V_W), BF16)],
        input_output_aliases=aliases,
        compiler_params=_cparams("parallel"),
        name="qk_prep",
    )(*args)


def _diff_attn_kernel(q0_ref, q1_ref, k_ref, v_ref, lq1_ref, lk1_ref, lq2_ref, lk2_ref, sg_ref, o_ref, *, lam_init):
    lam = (jnp.exp(jnp.sum(lq1_ref[...] * lk1_ref[...], axis=-1, keepdims=True))
           - jnp.exp(jnp.sum(lq2_ref[...] * lk2_ref[...], axis=-1, keepdims=True)) + lam_init)
    k = k_ref[0]
    vT = v_ref[0]

    def half(q):
        sT = lax.dot_general(k, q, (((1,), (1,)), ((), ())), preferred_element_type=F32)
        m = jnp.max(sT, axis=0, keepdims=True)
        p = jnp.exp2(sT - m)
        l = jnp.sum(p, axis=0, keepdims=True)
        return jnp.dot(vT, p.astype(BF16), preferred_element_type=F32) / l

    o = (half(q0_ref[0]) - lam * half(q1_ref[0])).T
    ms = jnp.mean(o * o, axis=-1, keepdims=True)
    o_ref[...] = (o * lax.rsqrt(ms + RMS_EPS) * sg_ref[...]) * (1.0 - lam_init)


def _diff_attn(qm, k_all, vT_all, lam_vecs, subln_g, lam_init, batch, lq, tq, lk):
    nq = lq // tq
    vec = lambda a: pl.BlockSpec(a.shape, lambda b, h, i: (0, 0))
    return pl.pallas_call(
        functools.partial(_diff_attn_kernel, lam_init=lam_init),
        grid=(batch, D_HEADS, nq),
        in_specs=[
            pl.BlockSpec((1, tq, D_V), lambda b, h, i: (0, b * nq + i, h)),
            pl.BlockSpec((1, tq, D_V), lambda b, h, i: (1, b * nq + i, h)),
            pl.BlockSpec((1, lk, D_V), lambda b, h, i: (b, 0, h)),
            pl.BlockSpec((1, D_V, lk), lambda b, h, i: (b, h, 0)),
            vec(lam_vecs[0]), vec(lam_vecs[1]), vec(lam_vecs[2]), vec(lam_vecs[3]), vec(subln_g),
        ],
        out_specs=pl.BlockSpec((tq, D_V), lambda b, h, i: (b * nq + i, h)),
        out_shape=jax.ShapeDtypeStruct((batch * lq, V_W), F32),
        compiler_params=_cparams("parallel", "parallel", "arbitrary"),
        name="diff_attn",
    )(qm, qm, k_all, vT_all, *lam_vecs, subln_g)


def _hy_pre_kernel(p_ref, pp_ref, pn_ref, sw_ref, sb_ref, x0_ref, u_ref, ub_ref, *, tiles_per_seq):
    i = pl.program_id(0)
    pos = i % tiles_per_seq
    z = p_ref[...].astype(F32)
    zp = jnp.where(pos == 0, 0.0, pp_ref[...].astype(F32)[HALO_ROWS - 1:HALO_ROWS, :])
    zn = jnp.where(pos == tiles_per_seq - 1, 0.0, pn_ref[...].astype(F32)[0:1, :])
    z_up, z_dn = _shift_rows(z, zp, zn)
    c = sw_ref[0:1, :] * z_up + sw_ref[1:2, :] * z + sw_ref[2:3, :] * z_dn + sb_ref[...]
    cw = C_WIDTH
    u = c[:, cw:2 * cw] * c[:, 2 * cw:3 * cw]
    x0_ref[...] = c[:, 0:cw]
    u_ref[...] = u
    ub_ref[...] = u.astype(BF16)


def _hy_pre(p, sw, sb, seq_len, tm):
    n = p.shape[0]
    nb8 = n // HALO_ROWS
    r8 = tm // HALO_ROWS
    prev = lambda i: jnp.maximum(i * r8 - 1, 0)
    nxt = lambda i: jnp.minimum((i + 1) * r8, nb8 - 1)
    o_spec = pl.BlockSpec((tm, C_WIDTH), lambda i: (i, 0))
    return pl.pallas_call(
        functools.partial(_hy_pre_kernel, tiles_per_seq=seq_len // tm),
        grid=(n // tm,),
        in_specs=[
            pl.BlockSpec((tm, HY_IN), lambda i: (i, 0)),
            pl.BlockSpec((HALO_ROWS, HY_IN), lambda i: (prev(i), 0)),
            pl.BlockSpec((HALO_ROWS, HY_IN), lambda i: (nxt(i), 0)),
            pl.BlockSpec((3, HY_IN), lambda i: (0, 0)),
            pl.BlockSpec((1, HY_IN), lambda i: (0, 0)),
        ],
        out_specs=[o_spec, o_spec, o_spec],
        out_shape=[jax.ShapeDtypeStruct((n, C_WIDTH), F32), jax.ShapeDtypeStruct((n, C_WIDTH), F32),
                   jax.ShapeDtypeStruct((n, C_WIDTH), BF16)],
        compiler_params=_cparams("parallel"),
        name="hyena_pre",
    )(p, p, p, sw, sb)


def _hy_filter_kernel(w1_ref, b1_ref, w2_ref, b2_ref, w3_ref, b3_ref, wo_ref, fr_ref, dl_ref, o_ref):
    L = o_ref.shape[0]
    fw = w1_ref.shape[0]
    bands = (HY_EMB - 1) // 2
    row = lax.broadcasted_iota(jnp.int32, (L, fw), 0).astype(F32)
    col = lax.broadcasted_iota(jnp.int32, (L, fw), 1)
    t = row / (L - 1.0)
    ang = (2.0 * math.pi) * row / L
    step = (bands - 1 - 1e-4) / (bands - 1)
    band = jnp.where(col <= bands, col - 1, col - 1 - bands).astype(F32)
    arg = (1e-4 + band * step) * ang
    feats = jnp.where(col == 0, t, jnp.where(col <= bands, jnp.cos(arg), jnp.where(col < HY_EMB, -jnp.sin(arg), 0.0)))
    fr = fr_ref[...]
    mm = lambda a, w: jnp.dot(a, w, precision=HI, preferred_element_type=F32)
    a = jnp.sin(fr * (mm(feats, w1_ref[...]) + b1_ref[...]))
    a = jnp.sin(fr * (mm(a, w2_ref[...]) + b2_ref[...]))
    a = jnp.sin(fr * (mm(a, w3_ref[...]) + b3_ref[...]))
    hf = mm(a, wo_ref[...])
    rowc = lax.broadcasted_iota(jnp.int32, (L, C_WIDTH), 0)
    tc = rowc.astype(F32) / (L - 1.0)
    decay = jnp.exp(-tc * dl_ref[...])
    h_fwd = hf[:, :C_WIDTH] * decay
    h_bwd = jnp.where(rowc == 0, 0.0, hf[:, C_WIDTH:] * decay)
    norm = jnp.sum(jnp.abs(h_fwd) + jnp.abs(h_bwd), axis=0, keepdims=True)
    o_ref[:, :C_WIDTH] = (h_fwd / norm).astype(BF16)
    o_ref[:, C_WIDTH:] = (h_bwd / norm).astype(BF16)


def _hy_filter(L, fp, deltas):
    full = lambda a: pl.BlockSpec(a.shape, lambda: (0,) * a.ndim)
    args = (*fp, deltas)
    return pl.pallas_call(
        _hy_filter_kernel,
        in_specs=[full(a) for a in args],
        out_specs=pl.BlockSpec((L, 2 * C_WIDTH), lambda: (0, 0)),
        out_shape=jax.ShapeDtypeStruct((L, 2 * C_WIDTH), BF16),
        compiler_params=pltpu.CompilerParams(vmem_limit_bytes=VMEM_LIMIT),
        name="hyena_filter",
    )(*args)


def _mm_kernel(a_ref, b_ref, o_ref):
    o_ref[...] = jnp.dot(a_ref[...], b_ref[...], preferred_element_type=F32)


def _matmul(a, b, tm):
    m, k = a.shape
    n = b.shape[1]
    return pl.pallas_call(
        _mm_kernel,
        grid=(m // tm,),
        in_specs=[pl.BlockSpec((tm, k), lambda i: (i, 0)), pl.BlockSpec((k, n), lambda i: (0, 0))],
        out_specs=pl.BlockSpec((tm, n), lambda i: (i, 0)),
        out_shape=jax.ShapeDtypeStruct((m, n), F32),
        compiler_params=_cparams("parallel"),
        name="dft_filter",
    )(a, b)


def _hy_conv_kernel(u_ref, wt_ref, wb_ref, w2t_ref, w2b_ref, at_ref, ab_ref, y_ref):
    j = pl.program_id(1)
    cw = C_WIDTH
    u = u_ref[...]
    uc = jnp.dot(wt_ref[...], u, preferred_element_type=F32)
    us = jnp.dot(wb_ref[...], u, preferred_element_type=F32)
    kre = at_ref[:, :cw] + at_ref[:, cw:]
    kim = ab_ref[:, cw:] - ab_ref[:, :cw]
    knyq = ab_ref[:, :cw] + ab_ref[:, cw:]
    row = lax.broadcasted_iota(jnp.int32, kre.shape, 0)
    special = jnp.logical_and(row == 0, j == 0)
    p2 = jnp.where(special, 0.0, kim)
    p4 = jnp.where(special, knyq, -kre)
    yt = (uc * kre + us * p2).astype(BF16)
    yb = (uc * p2 + us * p4).astype(BF16)
    L = y_ref.shape[0]
    tr = min(L, 512)

    def inverse_rows(rows, first):
        contrib = (jnp.dot(w2t_ref[rows, :], yt, preferred_element_type=F32)
                   + jnp.dot(w2b_ref[rows, :], yb, preferred_element_type=F32))
        y_ref[rows, :] = contrib if first else y_ref[rows, :] + contrib

    @pl.when(j == 0)
    def _():
        for c in range(L // tr):
            inverse_rows(slice(c * tr, (c + 1) * tr), True)

    @pl.when(j > 0)
    def _():
        for c in range(L // tr):
            inverse_rows(slice(c * tr, (c + 1) * tr), False)


def _hy_conv(ub, w, w2, a_spec, batch, L, rb):
    ns = L // rb
    return pl.pallas_call(
        _hy_conv_kernel,
        grid=(batch, ns),
        in_specs=[
            pl.BlockSpec((L, C_WIDTH), lambda b, j: (b, 0)),
            pl.BlockSpec((rb, L), lambda b, j: (j, 0)),
            pl.BlockSpec((rb, L), lambda b, j: (j + ns, 0)),
            pl.BlockSpec((L, rb), lambda b, j: (0, j)),
            pl.BlockSpec((L, rb), lambda b, j: (0, j + ns)),
            pl.BlockSpec((rb, 2 * C_WIDTH), lambda b, j: (j, 0)),
            pl.BlockSpec((rb, 2 * C_WIDTH), lambda b, j: (j + ns, 0)),
        ],
        out_specs=pl.BlockSpec((L, C_WIDTH), lambda b, j: (b, 0)),
        out_shape=jax.ShapeDtypeStruct((batch * L, C_WIDTH), F32),
        compiler_params=_cparams("parallel", "arbitrary"),
        name="hyena_conv",
    )(ub, w, w, w2, w2, a_spec, a_spec)


def _proj_cd_kernel(x0_ref, y_ref, u_ref, skip_ref, yd_ref, w_ref, x_ref, g1_ref, o_ref):
    yh = (x0_ref[...] * (y_ref[...] + u_ref[...] * skip_ref[...])).astype(BF16)
    acc = jnp.dot(yh, w_ref[:C_WIDTH, :], preferred_element_type=F32)
    acc += jnp.dot(yd_ref[...].astype(BF16), w_ref[C_WIDTH:, :], preferred_element_type=F32)
    o_ref[...] = x_ref[...] + g1_ref[0] * acc


def _proj_cd(x0, y, u, skip, yd, w_out, x, g1, row_of, tm):
    n, d = x.shape
    half = pl.BlockSpec((tm, C_WIDTH), lambda i: (i, 0))
    return pl.pallas_call(
        _proj_cd_kernel,
        grid=(n // tm,),
        in_specs=[
            half, half, half,
            pl.BlockSpec((1, C_WIDTH), lambda i: (0, 0)),
            half,
            pl.BlockSpec(w_out.shape, lambda i: (0, 0)),
            pl.BlockSpec((tm, d), lambda i: (i, 0)),
            pl.BlockSpec((1, 1, d), lambda i: (row_of(i), 0, 0)),
        ],
        out_specs=pl.BlockSpec((tm, d), lambda i: (i, 0)),
        out_shape=jax.ShapeDtypeStruct((n, d), F32),
        compiler_params=_cparams("parallel"),
        name="proj_cd",
    )(x0, y, u, skip, yd, w_out, x, g1)


def _dft_mats(L):
    m = 2 * L
    r = jnp.arange(L, dtype=jnp.int32)
    blk = math.gcd(L, 64)
    phase = lambda cols: ((r[:, None] * cols[None, :]) % m).astype(F32) * (2.0 * math.pi / m)
    a_hi = phase(jnp.arange(L // blk, dtype=jnp.int32) * blk)[:, :, None]
    a_lo = phase(jnp.arange(blk, dtype=jnp.int32))[:, None, :]
    c = (jnp.cos(a_hi) * jnp.cos(a_lo) - jnp.sin(a_hi) * jnp.sin(a_lo)).reshape(L, L)
    s = (jnp.sin(a_hi) * jnp.cos(a_lo) + jnp.cos(a_hi) * jnp.sin(a_lo)).reshape(L, L)
    alt = jnp.where(r % 2 == 0, 1.0, -1.0).astype(F32)
    is0 = (r == 0)[:, None]
    fwd = jnp.concatenate([c, jnp.where(is0, alt[None, :], s)], axis=0)
    inv_top = jnp.where(is0, 1.0 / m, c * (2.0 / m))
    inv_bot = jnp.where(is0, alt[None, :] / m, s * (-2.0 / m))
    inv = jnp.concatenate([inv_top, inv_bot], axis=0).T
    return fwd.astype(BF16), inv.astype(BF16)


def _peer_kernel(x_ref, sc_ref, sh_ref, g2_ref, ng_ref, wqT_ref, keys_ref, u0_ref, u_ref, vT_ref, o_ref,
                 h2T_s, sT_s, tops_s, th_s, iz_s, aT_s, bT_s, thT_s, act0_s, act1_s, w0_s, w1_s,
                 cutb0_s, cutb1_s, ab0_s, ab1_s, acc_s, *, nblk):
    e = pl.program_id(1)
    nk = keys_ref.shape[1]
    T = x_ref.shape[0]
    eb = u_ref.shape[0]
    nh = PEER_HEADS
    ntc = T // LANES
    neg = -jnp.inf

    def fill_rows(cutb, ab, blk):
        base = blk * SUBLANES if isinstance(blk, int) else pl.multiple_of(blk * SUBLANES, SUBLANES)
        for h in range(nh):
            cut8 = thT_s[h, pl.ds(base, SUBLANES), :]
            a8 = aT_s[h, pl.ds(base, SUBLANES), :]
            for ii in range(SUBLANES):
                cutb[h, ii] = jnp.broadcast_to(cut8[ii:ii + 1, :], (SUBLANES, T))
                ab[h, ii] = jnp.broadcast_to(a8[ii:ii + 1, :], (SUBLANES, T))

    @pl.when(e == 0)
    def _route():
        h2 = _norm_mod(x_ref[...], ng_ref[...], sc_ref[0], sh_ref[0])
        h2T = h2.T.astype(BF16)
        h2T_s[...] = h2T
        qT = jnp.dot(wqT_ref[...], h2T, preferred_element_type=F32).astype(BF16)
        pk = keys_ref.shape[2]
        for hp in range(2 * nh):
            sT_s[hp] = jnp.dot(keys_ref[hp], qT[hp * pk:(hp + 1) * pk, :], preferred_element_type=F32)

        def exchange(v, pairs):
            for i, j in pairs:
                v[i], v[j] = jnp.maximum(v[i], v[j]), jnp.minimum(v[i], v[j])

        def top_body(hp, carry):
            for tc in range(ntc):
                sl = slice(tc * LANES, (tc + 1) * LANES)
                v = [sT_s[hp, i * SUBLANES:(i + 1) * SUBLANES, sl] for i in range(PEER_TOPK)]
                exchange(v, _SORT_PAIRS)
                shift = SUBLANES // 2
                while shift >= 1:
                    w = [pltpu.roll(a, shift, 0) for a in v]
                    v = [jnp.maximum(v[i], w[PEER_TOPK - 1 - i]) for i in range(PEER_TOPK)]
                    exchange(v, _MERGE_PAIRS)
                    shift //= 2
                for i in range(PEER_TOPK):
                    tops_s[hp, i:i + 1, sl] = v[i][0:1]
            return carry

        lax.fori_loop(0, 2 * nh, top_body, 0)

        act0_s[...] = jnp.dot(u0_ref[...], h2T, preferred_element_type=F32)

        pairs = [(i, j) for i in range(PEER_TOPK) for j in range(PEER_TOPK // (i + 1))]
        for tc in range(ntc):
            sl = slice(tc * LANES, (tc + 1) * LANES)
            A = [jnp.concatenate([tops_s[2 * h, i:i + 1, sl] for h in range(nh)], axis=0) for i in range(PEER_TOPK)]
            B = [jnp.concatenate([tops_s[2 * h + 1, j:j + 1, sl] for h in range(nh)], axis=0)
                 for j in range(PEER_TOPK)]
            cand = [A[i] + B[j] for (i, j) in pairs]
            work = list(cand)
            tau = None
            for r in range(PEER_TOPK):
                tau = functools.reduce(jnp.maximum, work)
                if r < PEER_TOPK - 1:
                    work = [jnp.where(c >= tau, neg, c) for c in work]
            m0 = cand[0]
            z = functools.reduce(lambda a, b: a + b, [jnp.where(c >= tau, jnp.exp(c - m0), 0.0) for c in cand])
            iz = 1.0 / z
            for h in range(nh):
                iz_s[h, 0:1, sl] = iz[h:h + 1]
            for i in range(PEER_TOPK):
                row = [jnp.where(cand[n] >= tau, B[j], jnp.inf) for n, (ii, j) in enumerate(pairs) if ii == i]
                cut = functools.reduce(jnp.minimum, row)
                for h in range(nh):
                    th_s[h, i:i + 1, sl] = cut[h:h + 1]

        def head_body(h, carry):
            s0 = sT_s[2 * h]
            s1 = sT_s[2 * h + 1]
            a0 = tops_s[2 * h, 0:1, :]
            b0 = tops_s[2 * h + 1, 0:1, :]
            aT_s[h] = jnp.exp(s0 - a0) * (0.5 * iz_s[h, 0:1, :])
            bT_s[h] = jnp.exp(s1 - b0)
            th = jnp.full(s0.shape, jnp.inf, F32)
            for i in range(PEER_TOPK):
                th = jnp.where(s0 == tops_s[2 * h, i:i + 1, :], th_s[h, i:i + 1, :], th)
            thT_s[h] = th
            return carry

        lax.fori_loop(0, nh, head_body, 0)

        w1_s[...] = jnp.zeros(w1_s.shape, BF16)
        acc_s[...] = jnp.zeros(acc_s.shape, F32)
        fill_rows(cutb0_s, ab0_s, 0)

    def stage(act_w, act_r, w_w, w_r, cutb_w, ab_w, cutb_r, ab_r):
        nrc = nk // GATE_ROWS
        nvr = GATE_ROWS // SUBLANES
        mrows = eb // ntc
        drows = acc_s.shape[0] // ntc
        def mxu_slice(tc):
            ru = slice(tc * mrows, (tc + 1) * mrows)
            act_w[ru, :] = jnp.dot(u_ref[ru, :], h2T_s[...], preferred_element_type=F32)
            rd = slice(tc * drows, (tc + 1) * drows)
            acc_s[rd, :] += jnp.dot(vT_ref[rd, :], w_r[...], preferred_element_type=F32)

        for tc in range(ntc):
            sl = slice(tc * LANES, (tc + 1) * LANES)
            for rc in range(nrc):
                g = [[jnp.zeros((SUBLANES, LANES), F32) for _ in range(nvr)] for _ in range(SUBLANES)]
                for h in range(nh):
                    s1c = [sT_s[2 * h + 1, rc * GATE_ROWS + v * SUBLANES:rc * GATE_ROWS + (v + 1) * SUBLANES, sl]
                           for v in range(nvr)]
                    bc = [bT_s[h, rc * GATE_ROWS + v * SUBLANES:rc * GATE_ROWS + (v + 1) * SUBLANES, sl]
                          for v in range(nvr)]
                    for ii in range(SUBLANES):
                        cut = cutb_r[h, ii, :, sl]
                        a = ab_r[h, ii, :, sl]
                        for v in range(nvr):
                            g[ii][v] = g[ii][v] + jnp.where(s1c[v] >= cut, a * bc[v], 0.0)
                for ii in range(SUBLANES):
                    rows = slice(ii * nk + rc * GATE_ROWS, ii * nk + (rc + 1) * GATE_ROWS)
                    gi = jnp.concatenate(g[ii], axis=0)
                    act = act_r[rows, sl]
                    w_w[rows, sl] = (gi * (act * (1.0 + lax.erf(act * (2.0 ** -0.5))))).astype(BF16)
            mxu_slice(tc)
        fill_rows(cutb_w, ab_w, jnp.minimum(e + 1, nblk - 1))

    @pl.when(jnp.logical_and(e < nblk, e % 2 == 0))
    def _():
        stage(act1_s, act0_s, w0_s, w1_s, cutb1_s, ab1_s, cutb0_s, ab0_s)

    @pl.when(jnp.logical_and(e < nblk, e % 2 == 1))
    def _():
        stage(act0_s, act1_s, w1_s, w0_s, cutb0_s, ab0_s, cutb1_s, ab1_s)

    @pl.when(e == nblk)
    def _():
        w_last = w0_s if (nblk - 1) % 2 == 0 else w1_s
        acc = acc_s[...] + jnp.dot(vT_ref[...], w_last[...], preferred_element_type=F32)
        o_ref[...] = x_ref[...] + g2_ref[0] * acc.T


def _peer(x, sc, sh, g2, ng, wqT, keys, u_b, vT_b, row_of, T, eb):
    n, d = x.shape
    n_exp = u_b.shape[0]
    nk = keys.shape[1]
    nh = PEER_HEADS
    modspec = pl.BlockSpec((1, 1, d), lambda i, e: (row_of(i), 0, 0))
    nblk = n_exp // eb
    assert eb == nk * SUBLANES and nk % GATE_ROWS == 0 and T % LANES == 0 and nk == SUBLANES * PEER_TOPK
    once = pl.Buffered(1)
    return pl.pallas_call(
        functools.partial(_peer_kernel, nblk=nblk),
        grid=(n // T, nblk + 1),
        in_specs=[
            pl.BlockSpec((T, d), lambda i, e: (i, 0)),
            modspec, modspec, modspec,
            pl.BlockSpec((1, d), lambda i, e: (0, 0)),
            pl.BlockSpec(wqT.shape, lambda i, e: (0, 0), pipeline_mode=once),
            pl.BlockSpec(keys.shape, lambda i, e: (0, 0, 0), pipeline_mode=once),
            pl.BlockSpec((eb, d), lambda i, e: (0, 0), pipeline_mode=once),
            pl.BlockSpec((eb, d), lambda i, e: (jnp.minimum(e + 1, nblk - 1), 0)),
            pl.BlockSpec((d, eb), lambda i, e: (0, jnp.clip(e - 1, 0, nblk - 1))),
        ],
        out_specs=pl.BlockSpec((T, d), lambda i, e: (i, 0)),
        out_shape=jax.ShapeDtypeStruct((n, d), F32),
        scratch_shapes=[
            pltpu.VMEM((d, T), BF16),
            pltpu.VMEM((2 * nh, nk, T), F32),
            pltpu.VMEM((2 * nh, PEER_TOPK, T), F32),
            pltpu.VMEM((nh, PEER_TOPK, T), F32),
            pltpu.VMEM((nh, SUBLANES, T), F32),
            pltpu.VMEM((nh, nk, T), F32),
            pltpu.VMEM((nh, nk, T), F32),
            pltpu.VMEM((nh, nk, T), F32),
            pltpu.VMEM((eb, T), F32),
            pltpu.VMEM((eb, T), F32),
            pltpu.VMEM((eb, T), BF16),
            pltpu.VMEM((eb, T), BF16),
            pltpu.VMEM((nh, SUBLANES, SUBLANES, T), F32),
            pltpu.VMEM((nh, SUBLANES, SUBLANES, T), F32),
            pltpu.VMEM((nh, SUBLANES, SUBLANES, T), F32),
            pltpu.VMEM((nh, SUBLANES, SUBLANES, T), F32),
            pltpu.VMEM((d, T), F32),
        ],
        compiler_params=_cparams("parallel", "arbitrary"),
        name="peer",
    )(x, sc, sh, g2, ng, wqT, keys, u_b, u_b, vT_b)


def _pick(n, pref):
    t = pref
    while n % t:
        t //= 2
    return t


def kernel(x, c, ctx, c_ctx, w_mod, b_mod, norm1_g, norm2_g, ab_w_in, a_ws, a_bs, a_vnorm_g, b_conv_w, ab_w_out,
           cd_w_in, c_short_w, c_short_b, c_filt_w1, c_filt_b1, c_filt_w2, c_filt_b2, c_filt_w3, c_filt_b3,
           c_filt_wout, c_filt_freq, c_skip, d_q_norm_g, d_k_norm_g, d_lambda_q1, d_lambda_k1, d_lambda_q2,
           d_lambda_k2, d_subln_g, cd_w_out, peer_wq, peer_keys, peer_u, peer_v):
    B, L, D = x.shape
    Lc = ctx.shape[1]
    depth = w_mod.shape[0]
    n_lat, n_ctx = B * L, B * Lc

    tm_l, tm_c = _pick(L, 512), _pick(Lc, 256)
    tm_kv = _pick(math.gcd(L, Lc), 256)
    tq_l, tq_c = _pick(L, 1024), _pick(Lc, 512)
    tp_l, tp_c = _pick(L, 512), _pick(n_ctx, 512)
    eb = 1024

    mod_rows = 2 * SUBLANES
    s_rows = jnp.zeros((mod_rows, D), F32).at[:B].set(c).at[B].set(c_ctx)
    mod = _mod_all(s_rows, w_mod, b_mod).reshape(depth, mod_rows, N_MOD, 1, D)

    xl = x.reshape(n_lat, D)
    xc = ctx.reshape(n_ctx, D)

    def lat_row(t):
        return lambda i: (i * t) // L

    ctx_row = lambda i: B

    col = jnp.arange(QK_W)
    inv = ROPE_BASE ** (-jnp.arange(0, ROPE_AXIS_DIM, 2, dtype=F32) / ROPE_AXIS_DIM)
    pos = jnp.arange(L)
    rows_f = (pos // GRID_W).astype(F32)
    cols_f = (pos % GRID_W).astype(F32)
    is_row_part = (col % D_QK) < ROPE_AXIS_DIM
    ang = jnp.where(is_row_part[None, :], rows_f[:, None], cols_f[:, None]) * inv[col % (ROPE_AXIS_DIM // 2)][None, :]
    cos_l = jnp.cos(ang)
    sin_l = jnp.where(((col % ROPE_AXIS_DIM) < ROPE_AXIS_DIM // 2)[None, :], -jnp.sin(ang), jnp.sin(ang))
    cos_c = jnp.ones((tm_kv, QK_W), F32)
    sin_c = jnp.zeros((tm_kv, QK_W), F32)

    deltas = jnp.abs(jnp.linspace(math.log(HY_TARGET) / HY_DECAY_LONG_PCT, math.log(HY_TARGET) / HY_DECAY_SHORT_PCT,
                                  C_WIDTH, dtype=F32))[None, :]

    for l in range(depth):
        last = l == depth - 1
        odd = l % 2 == 1
        i = l // 2
        m = lambda k: mod[l, :, k]
        sh1, sc1, g1, sh2, sc2, g2 = (m(k) for k in range(N_MOD))
        n1 = norm1_g[l][None, :]
        need_ctx = (not last) or odd

        if not odd:
            w_in = ab_w_in[i].astype(BF16)
            w_out = ab_w_out[i].astype(BF16)
            ws = a_ws[i].astype(BF16)
            bias_full = jnp.repeat(a_bs[i].T, A_HEAD_DIM, axis=1)
            vg = a_vnorm_g[i][None, :]
            cw = b_conv_w[i]
            p = _norm_mod_matmul(xl, sc1, sh1, n1, w_in, lat_row(tm_l), tm_l)
            xl = _mix_ab(p, xl, g1, ws, bias_full, vg, cw, w_out, lat_row(tm_l), L, tm_l)
            if not last:
                pc = _norm_mod_matmul(xc, sc1, sh1, n1, w_in, ctx_row, tm_c)
                xc = _mix_ab(pc, xc, g1, ws, bias_full, vg, cw, w_out, ctx_row, Lc, tm_c)
        else:
            lam_init = 0.8 - 0.6 * math.exp(-0.3 * l)
            w_in = cd_w_in[i].astype(BF16)
            w_out = cd_w_out[i].astype(BF16)
            qg = jnp.tile(d_q_norm_g[i], QK_W // D_QK)[None, :]
            kg = jnp.tile(d_k_norm_g[i], QK_W // D_QK)[None, :]
            lam_vecs = tuple(v[i][None, :] for v in (d_lambda_q1, d_lambda_k1, d_lambda_q2, d_lambda_k2))
            sg = d_subln_g[i][None, :]
            fw = LANES
            padc = lambda a: jnp.pad(a, ((0, 0), (0, fw - a.shape[1])))
            padr = lambda a: jnp.pad(a, ((0, fw - a.shape[0]), (0, 0)))
            fp = (padc(padr(c_filt_w1[i])), padc(c_filt_b1[i][None, :]), padc(padr(c_filt_w2[i])),
                  padc(c_filt_b2[i][None, :]), padc(padr(c_filt_w3[i])), padc(c_filt_b3[i][None, :]),
                  padr(c_filt_wout[i]), padc(c_filt_freq[i][None, :]))
            sw, sb, skip = c_short_w[i], c_short_b[i][None, :], c_skip[i][None, :]

            def hyena(pp, batch, seq, tm):
                x0, u, ub = _hy_pre(pp, sw, sb, seq, tm)
                fwd, invm = _dft_mats(seq)
                filt = _hy_filter(seq, fp, deltas)
                spec = _matmul(fwd, filt, _pick(2 * seq, 512))
                y = _hy_conv(ub, fwd, invm, spec, batch, seq, _pick(seq, 256))
                return x0, y, u

            pc = _norm_mod_matmul(xc, sc1, sh1, n1, w_in, ctx_row, tm_c)
            kv_blocks = (Lc + L) // tm_kv
            ctx_kv_row = lambda t: (t // (Lc // tm_kv)) * kv_blocks + t % (Lc // tm_kv)
            lat_kv_row = lambda t: (t // (L // tm_kv)) * kv_blocks + Lc // tm_kv + t % (L // tm_kv)
            qmc, k_all, v_all = _qk_prep(pc, cos_c, sin_c, qg, kg, lambda t: 0, tm_kv, B * (Lc + L), ctx_kv_row)
            p = _norm_mod_matmul(xl, sc1, sh1, n1, w_in, lat_row(tm_l), tm_l)
            qm, k_all, v_all = _qk_prep(p, cos_l, sin_l, qg, kg, lambda t: t % (L // tm_kv), tm_kv, B * (Lc + L),
                                        lat_kv_row, (k_all, v_all))
            k_all = k_all.reshape(B, Lc + L, QK_W)
            vT_all = v_all.reshape(B, Lc + L, V_W).transpose(0, 2, 1)
            yd = _diff_attn(qm, k_all, vT_all, lam_vecs, sg, lam_init, B, L, tq_l, Lc + L)
            x0, y, u = hyena(p, B, L, tm_l)
            xl = _proj_cd(x0, y, u, skip, yd, w_out, xl, g1, lat_row(tm_l), tm_l)
            if not last:
                ydc = _diff_attn(qmc, k_all, vT_all, lam_vecs, sg, lam_init, B, Lc, tq_c, Lc)
                x0c, yc, uc = hyena(pc, B, Lc, tm_c)
                xc = _proj_cd(x0c, yc, uc, skip, ydc, w_out, xc, g1, ctx_row, tm_c)

        n2 = norm2_g[l][None, :]
        wqT = peer_wq[l].T.astype(BF16)
        keys = peer_keys[l].reshape(2 * PEER_HEADS, peer_keys.shape[3], peer_keys.shape[4]).astype(BF16)
        u_b = peer_u[l].astype(BF16)
        vT_b = peer_v[l].T.astype(BF16)
        xl = _peer(xl, sc2, sh2, g2, n2, wqT, keys, u_b, vT_b, lat_row(tp_l), tp_l, eb)
        if not last:
            xc = _peer(xc, sc2, sh2, g2, n2, wqT, keys, u_b, vT_b, ctx_row, tp_c, eb)

    return xl.reshape(B, L, D)
```
